```python
import math
import jax, jax.numpy as jnp
from jax import lax
import numpy as np

D_MODEL = 1024
BATCH = 8
SEQ = 4096
DEPTH = 4

CTX_LEN = 256
GRID_W = 64

D_HYENA = 512
D_SCONV = 512
NA_HEADS = 8
NA_HEAD_DIM = 64
D_NA = NA_HEADS * NA_HEAD_DIM
N_BRANCH = 3
NA_WIN_ROWS = 8
NA_WIN_COLS = 16
NA_QBLK = 16
NA_KBLK = NA_QBLK + NA_WIN_COLS
HY_BANDS = 16
HY_EMB = 1 + 2 * HY_BANDS
HY_FILTER_DIM = 64
HY_FAST_DECAY = 0.3
HY_SLOW_DECAY = 1.5
HY_TARGET = 1e-2
N_EXPERTS = 64
N_GROUPS = 8
TOPK_GROUPS = 4
TOP_K = 8
D_EXPERT = 256
D_SHARED = 256
ROUTED_SCALE = 2.5
MOE_BLOCK = 256
LN_EPS = 1e-5
NEG_INF = -1e30

OFF_HY = 0
OFF_SC = OFF_HY + 3 * D_HYENA
OFF_NA = OFF_SC + 3 * D_SCONV
OFF_GATE = OFF_NA + 3 * D_NA
D_IN_PROJ = OFF_GATE + N_BRANCH * D_MODEL

kernel_name = 'hybrid_hyena_shortconv_natten_moe_trunk'


def _deepnorm_alpha():
    return (2.0 * DEPTH) ** 0.25


def _deepnorm_beta():
    return (8.0 * DEPTH) ** -0.25


def layer_norm(x, g, b):
    xf = x.astype(jnp.float32)
    mu = jnp.mean(xf, -1, keepdims=True)
    var = jnp.mean(jnp.square(xf - mu), -1, keepdims=True)
    y = (xf - mu) * lax.rsqrt(var + LN_EPS) * g.astype(jnp.float32) + b.astype(jnp.float32)
    return y.astype(x.dtype)


def dwconv3(u, w):
    up = jnp.pad(u, ((0, 0), (1, 1), (0, 0)))
    return up[:, :-2] * w[0] + up[:, 1:-1] * w[1] + up[:, 2:] * w[2]


def hyena_filter(L, w1, b1, w2, b2, w3, sin_freq):
    f32 = jnp.float32
    t = jnp.linspace(0.0, 1.0, L, dtype=f32)[:, None]
    ang = (2.0 * math.pi / L) * jnp.arange(L, dtype=f32)[:, None]
    bands = jnp.linspace(1e-4, HY_BANDS - 1, HY_BANDS, dtype=f32)[None, :]
    feats = jnp.concatenate([t, jnp.cos(bands * ang), -jnp.sin(bands * ang)], -1)
    z = jnp.sin(sin_freq[0].astype(f32) * (feats @ w1.astype(f32) + b1.astype(f32)))
    z = jnp.sin(sin_freq[1].astype(f32) * (z @ w2.astype(f32) + b2.astype(f32)))
    h = z @ w3.astype(f32)
    deltas = jnp.abs(jnp.linspace(math.log(HY_TARGET) / HY_SLOW_DECAY, math.log(HY_TARGET) / HY_FAST_DECAY, D_HYENA, dtype=f32))
    decay = jnp.exp(-t * deltas[None, :])
    h_fwd = h[:, :D_HYENA] * decay
    h_bwd = h[:, D_HYENA:] * decay
    return jnp.concatenate([h_fwd, jnp.zeros((1, D_HYENA), f32), h_bwd[:0:-1]], 0)


def hyena_mix(u3, conv_w, conv_b, w1, b1, w2, b2, w3, sin_freq, bias_d):
    L = u3.shape[1]
    uc = dwconv3(u3, conv_w) + conv_b
    x0, x1, v = jnp.split(uc, 3, axis=-1)
    z = v * x1
    g = hyena_filter(L, w1, b1, w2, b2, w3, sin_freq)
    n = 2 * L
    zf = jnp.fft.rfft(z.astype(jnp.float32), n=n, axis=1)
    y = jnp.fft.irfft(zf * jnp.fft.rfft(g, axis=0)[None], n=n, axis=1)[:, :L]
    return x0 * (y.astype(z.dtype) + z * bias_d)


def short_conv_mix(u3, conv_w):
    bg, cg, xs = jnp.split(u3, 3, axis=-1)
    return bg * dwconv3(cg * xs, conv_w)


def _heads(t):
    b, l, _ = t.shape
    return t.reshape(b, l, NA_HEADS, NA_HEAD_DIM).transpose(0, 2, 1, 3)


def context_attention(q, k, v):
    b, l, _ = q.shape
    qh, kh, vh = _heads(q), _heads(k), _heads(v)
    s = jnp.einsum('bhqd,bhkd->bhqk', qh, kh, preferred_element_type=jnp.float32) * (NA_HEAD_DIM ** -0.5)
    p = jax.nn.softmax(s, axis=-1).astype(vh.dtype)
    o = jnp.einsum('bhqk,bhkd->bhqd', p, vh)
    return o.transpose(0, 2, 1, 3).reshape(b, l, D_NA)


def _na_col_layout():
    wc = min(NA_WIN_COLS, GRID_W)
    qcol = np.arange(GRID_W)
    cstart = np.clip(qcol - wc // 2, 0, GRID_W - wc)
    ncb = GRID_W // NA_QBLK
    kbw = min(NA_KBLK, GRID_W)
    kb = np.minimum(cstart[::NA_QBLK], GRID_W - kbw)
    kcol = kb[:, None] + np.arange(kbw)[None, :]
    qc = qcol.reshape(ncb, NA_QBLK)
    cs = cstart.reshape(ncb, NA_QBLK)
    inwin = (kcol[:, None, :] >= cs[:, :, None]) & (kcol[:, None, :] < cs[:, :, None] + wc)
    rel = np.clip(kcol[:, None, :] - qc[:, :, None] + NA_WIN_COLS - 1, 0, 2 * NA_WIN_COLS - 2)
    return ncb, kbw, kcol, inwin, rel


def neighbourhood_attention(q, k, v, k_ctx, v_ctx, rpb):
    b, s, _ = q.shape
    rows = s // GRID_W
    wr = min(NA_WIN_ROWS, rows)
    ncb, kbw, kcol, inwin, rel = _na_col_layout()
    scale = NA_HEAD_DIM ** -0.5
    n_loc = wr * kbw

    def grid(t):
        return t.reshape(b, rows, GRID_W, NA_HEADS, NA_HEAD_DIM).transpose(1, 0, 3, 2, 4)

    qg = grid(q)
    kg = grid(k)[:, :, :, kcol]
    vg = grid(v)[:, :, :, kcol]
    kc, vc = _heads(k_ctx), _heads(v_ctx)
    bias_cols = rpb[:, :, rel]
    mask = jnp.asarray(np.broadcast_to(inwin[:, :, None, :], (ncb, NA_QBLK, wr, kbw)).reshape(ncb, NA_QBLK, n_loc))

    def row_step(r):
        rs = jnp.clip(r - wr // 2, 0, rows - wr)
        kr = lax.dynamic_slice_in_dim(kg, rs, wr, axis=0).transpose(1, 2, 3, 0, 4, 5).reshape(b, NA_HEADS, ncb, n_loc, NA_HEAD_DIM)
        vr = lax.dynamic_slice_in_dim(vg, rs, wr, axis=0).transpose(1, 2, 3, 0, 4, 5).reshape(b, NA_HEADS, ncb, n_loc, NA_HEAD_DIM)
        qr = qg[r].reshape(b, NA_HEADS, ncb, NA_QBLK, NA_HEAD_DIM)
        ridx = rs - r + jnp.arange(wr) + NA_WIN_ROWS - 1
        bias = bias_cols[:, ridx].transpose(0, 2, 3, 1, 4).reshape(NA_HEADS, ncb, NA_QBLK, n_loc)
        s_loc = jnp.einsum('bhnqd,bhnkd->bhnqk', qr, kr, preferred_element_type=jnp.float32) * scale + bias.astype(jnp.float32)
        s_loc = jnp.where(mask, s_loc, NEG_INF)
        s_ctx = jnp.einsum('bhnqd,bhcd->bhnqc', qr, kc, preferred_element_type=jnp.float32) * scale
        p = jax.nn.softmax(jnp.concatenate([s_loc, s_ctx], -1), axis=-1).astype(vr.dtype)
        o = jnp.einsum('bhnqk,bhnkd->bhnqd', p[..., :n_loc], vr) + jnp.einsum('bhnqc,bhcd->bhnqd', p[..., n_loc:], vc)
        return o.reshape(b, NA_HEADS, GRID_W, NA_HEAD_DIM)

    o = lax.map(row_step, jnp.arange(rows))
    return o.transpose(1, 0, 3, 2, 4).reshape(b, s, D_NA)


def mixer_output(p, y_attn, hy, sc_conv_w, hy_proj, sc_proj, na_proj, w_o):
    y_hy = hyena_mix(p[..., OFF_HY:OFF_SC], *hy)
    y_sc = short_conv_mix(p[..., OFF_SC:OFF_NA], sc_conv_w)
    g_hy, g_sc, g_na = jnp.split(jax.nn.sigmoid(p[..., OFF_GATE:]), 3, axis=-1)
    merged = g_hy * (y_hy @ hy_proj) + g_sc * (y_sc @ sc_proj) + g_na * (y_attn @ na_proj)
    return merged @ w_o


def moe_ffn(h, router_w, router_b, w1, w3, w2, s1, s3, s2):
    t_tok, d = h.shape
    scores = jax.nn.sigmoid((h @ router_w).astype(jnp.float32))
    sel = scores + router_b.astype(jnp.float32)
    per_group = N_EXPERTS // N_GROUPS
    group_score = lax.top_k(sel.reshape(t_tok, N_GROUPS, per_group), 2)[0].sum(-1)
    _, top_groups = lax.top_k(group_score, TOPK_GROUPS)
    group_mask = (top_groups[..., None] == jnp.arange(N_GROUPS)).any(axis=1)
    expert_mask = jnp.repeat(group_mask, per_group, axis=1)
    _, top_e = lax.top_k(jnp.where(expert_mask, sel, -jnp.inf), TOP_K)
    w = jnp.take_along_axis(scores, top_e, axis=1)
    w = w / jnp.sum(w, -1, keepdims=True) * ROUTED_SCALE
    tk = t_tok * TOP_K
    flat_e = top_e.reshape(tk)
    order = jnp.argsort(flat_e)
    e_s = flat_e[order]
    tok_s = (order // TOP_K).astype(jnp.int32)
    w_s = w.reshape(tk)[order]
    counts = jnp.bincount(flat_e, length=N_EXPERTS)
    padded = (counts + MOE_BLOCK - 1) // MOE_BLOCK * MOE_BLOCK
    start = jnp.cumsum(counts) - counts
    pstart = jnp.cumsum(padded) - padded
    dest = pstart[e_s] + jnp.arange(tk, dtype=jnp.int32) - start[e_s]
    n_blocks = -(-tk // MOE_BLOCK) + N_EXPERTS
    tok_buf = jnp.full((n_blocks * MOE_BLOCK,), t_tok, jnp.int32).at[dest].set(tok_s)
    w_buf = jnp.zeros((n_blocks * MOE_BLOCK,), h.dtype).at[dest].set(w_s.astype(h.dtype))
    blk_expert = jnp.minimum(jnp.searchsorted(jnp.cumsum(padded) // MOE_BLOCK, jnp.arange(n_blocks), side='right'), N_EXPERTS - 1)
    h_pad = jnp.concatenate([h, jnp.zeros((1, d), h.dtype)], 0)

    def block_step(acc, blk):
        tb, wb, e = blk
        xb = h_pad[tb]
        yb = (jax.nn.silu(xb @ w1[e]) * (xb @ w3[e])) @ w2[e]
        return acc.at[tb].add(yb * wb[:, None]), None

    routed, _ = lax.scan(block_step, jnp.zeros((t_tok + 1, d), h.dtype),
                         (tok_buf.reshape(n_blocks, MOE_BLOCK), w_buf.reshape(n_blocks, MOE_BLOCK), blk_expert))
    shared = (jax.nn.silu(h @ s1) * (h @ s3)) @ s2
    return routed[:t_tok] + shared


def setup_inputs(seed: int = 0) -> dict:
    key = jax.random.key(seed)
    ks = iter(jax.random.split(key, 40))
    beta = _deepnorm_beta()
    L = DEPTH

    def nrm(shape, scale):
        return jax.random.normal(next(ks), shape, jnp.float32) * scale

    return {
        'x': nrm((BATCH, SEQ, D_MODEL), 1.0),
        'c': nrm((BATCH, D_MODEL), 1.0),
        'ctx': nrm((BATCH, CTX_LEN, D_MODEL), 1.0),
        'c_ctx': nrm((D_MODEL,), 1.0),
        'w_ada': nrm((L, D_MODEL, 6 * D_MODEL), 0.5 * D_MODEL ** -0.5),
        'b_ada': nrm((L, 6 * D_MODEL), 0.01),
        'w_in': nrm((L, D_MODEL, D_IN_PROJ), D_MODEL ** -0.5),
        'hy_conv_w': nrm((L, 3, 3 * D_HYENA), 3 ** -0.5),
        'hy_conv_b': nrm((L, 3 * D_HYENA), 0.01),
        'hy_w1': nrm((L, HY_EMB, HY_FILTER_DIM), HY_EMB ** -0.5),
        'hy_b1': nrm((L, HY_FILTER_DIM), 0.01),
        'hy_w2': nrm((L, HY_FILTER_DIM, HY_FILTER_DIM), HY_FILTER_DIM ** -0.5),
        'hy_b2': nrm((L, HY_FILTER_DIM), 0.01),
        'hy_w3': nrm((L, HY_FILTER_DIM, 2 * D_HYENA), 0.03 * HY_FILTER_DIM ** -0.5),
        'hy_sin_freq': 1.0 + nrm((L, 2, HY_FILTER_DIM), 0.01),
        'hy_bias_d': nrm((L, D_HYENA), 0.1),
        'hy_proj': nrm((L, D_HYENA, D_MODEL), beta * D_HYENA ** -0.5),
        'sc_conv_w': nrm((L, 3, D_SCONV), 3 ** -0.5),
        'sc_proj': nrm((L, D_SCONV, D_MODEL), beta * D_SCONV ** -0.5),
        'na_rpb': nrm((L, NA_HEADS, 2 * NA_WIN_ROWS - 1, 2 * NA_WIN_COLS - 1), 0.02),
        'na_proj': nrm((L, D_NA, D_MODEL), beta * D_NA ** -0.5),
        'w_o': nrm((L, D_MODEL, D_MODEL), beta * D_MODEL ** -0.5),
        'ln1_g': 1.0 + nrm((L, D_MODEL), 0.01),
        'ln1_b': nrm((L, D_MODEL), 0.01),
        'ln2_g': 1.0 + nrm((L, D_MODEL), 0.01),
        'ln2_b': nrm((L, D_MODEL), 0.01),
        'moe_router': nrm((L, D_MODEL, N_EXPERTS), D_MODEL ** -0.5),
        'moe_bias': nrm((L, N_EXPERTS), 0.01),
        'moe_w1': nrm((L, N_EXPERTS, D_MODEL, D_EXPERT), D_MODEL ** -0.5),
        'moe_w3': nrm((L, N_EXPERTS, D_MODEL, D_EXPERT), D_MODEL ** -0.5),
        'moe_w2': nrm((L, N_EXPERTS, D_EXPERT, D_MODEL), beta * D_EXPERT ** -0.5),
        'sh_w1': nrm((L, D_MODEL, D_SHARED), D_MODEL ** -0.5),
        'sh_w3': nrm((L, D_MODEL, D_SHARED), D_MODEL ** -0.5),
        'sh_w2': nrm((L, D_SHARED, D_MODEL), beta * D_SHARED ** -0.5),
    }


def reference(x, c, ctx, c_ctx, w_ada, b_ada, w_in, hy_conv_w, hy_conv_b, hy_w1, hy_b1, hy_w2, hy_b2, hy_w3,
              hy_sin_freq, hy_bias_d, hy_proj, sc_conv_w, sc_proj, na_rpb, na_proj, w_o, ln1_g, ln1_b, ln2_g, ln2_b,
              moe_router, moe_bias, moe_w1, moe_w3, moe_w2, sh_w1, sh_w3, sh_w2):
    alpha = _deepnorm_alpha()
    b, s, d = x.shape
    n_lat = b * s
    cond = jax.nn.silu(c)
    cond_ctx = jax.nn.silu(c_ctx)
    xl, xc = x, ctx
    for i in range(DEPTH):
        last = i == DEPTH - 1
        mod_l = jnp.split((cond @ w_ada[i] + b_ada[i])[:, None, :], 6, axis=-1)
        mod_c = jnp.split(cond_ctx @ w_ada[i] + b_ada[i], 6, axis=-1)
        hy = (hy_conv_w[i], hy_conv_b[i], hy_w1[i], hy_b1[i], hy_w2[i], hy_b2[i], hy_w3[i], hy_sin_freq[i], hy_bias_d[i])

        hl = xl * (1 + mod_l[1]) + mod_l[0]
        hc = xc * (1 + mod_c[1]) + mod_c[0]
        pl = hl @ w_in[i]
        if last:
            k_c, v_c = jnp.split(hc @ w_in[i][:, OFF_NA + D_NA:OFF_GATE], 2, axis=-1)
        else:
            pc = hc @ w_in[i]
            q_c, k_c, v_c = jnp.split(pc[..., OFF_NA:OFF_GATE], 3, axis=-1)
        q_l, k_l, v_l = jnp.split(pl[..., OFF_NA:OFF_GATE], 3, axis=-1)
        att_l = neighbourhood_attention(q_l, k_l, v_l, k_c, v_c, na_rpb[i])
        out_l = mixer_output(pl, att_l, hy, sc_conv_w[i], hy_proj[i], sc_proj[i], na_proj[i], w_o[i])
        xl = layer_norm(alpha * xl + mod_l[2] * out_l, ln1_g[i], ln1_b[i])
        if not last:
            att_c = context_attention(q_c, k_c, v_c)
            out_c = mixer_output(pc, att_c, hy, sc_conv_w[i], hy_proj[i], sc_proj[i], na_proj[i], w_o[i])
            xc = layer_norm(alpha * xc + mod_c[2] * out_c, ln1_g[i], ln1_b[i])

        tokens = (xl * (1 + mod_l[4]) + mod_l[3]).reshape(n_lat, d)
        if not last:
            hc2 = xc * (1 + mod_c[4]) + mod_c[3]
            tokens = jnp.concatenate([tokens, hc2.reshape(-1, d)], 0)
        f = moe_ffn(tokens, moe_router[i], moe_bias[i], moe_w1[i], moe_w3[i], moe_w2[i], sh_w1[i], sh_w3[i], sh_w2[i])
        xl = layer_norm(alpha * xl + mod_l[5] * f[:n_lat].reshape(b, s, d), ln2_g[i], ln2_b[i])
        if not last:
            xc = layer_norm(alpha * xc + mod_c[5] * f[n_lat:].reshape(xc.shape), ln2_g[i], ln2_b[i])
    return xl
```

```python
import functools
import math

import numpy as np
import jax
import jax.numpy as jnp
from jax import lax
from jax.experimental import pallas as pl
from jax.experimental.pallas import tpu as pltpu

F32 = jnp.float32
BF16 = jnp.bfloat16
I32 = jnp.int32
HIGHEST = lax.Precision.HIGHEST

D_MODEL = 1024
BATCH = 8
SEQ = 4096
DEPTH = 4
CTX_LEN = 256
GRID_W = 64
D_HYENA = 512
D_SCONV = 512
NA_HEADS = 8
NA_HEAD_DIM = 64
D_NA = NA_HEADS * NA_HEAD_DIM
NA_WIN_ROWS = 8
NA_WIN_COLS = 16
HY_BANDS = 16
HY_EMB = 1 + 2 * HY_BANDS
HY_FILTER_DIM = 64
HY_FAST_DECAY = 0.3
HY_SLOW_DECAY = 1.5
HY_TARGET = 1e-2
N_EXPERTS = 64
N_GROUPS = 8
TOPK_GROUPS = 4
TOP_K = 8
D_EXPERT = 256
D_SHARED = 256
ROUTED_SCALE = 2.5
LN_EPS = 1e-5
NEG_INF = -1e30
ALPHA = (2.0 * DEPTH) ** 0.25

N_LAT = BATCH * SEQ
N_CTX = BATCH * CTX_LEN
N_TOK = N_LAT + N_CTX

LANE = 128
SUBLANE = 8
VMEM_LIMIT = 56 * 1024 * 1024

TOK_TILE = 256
NA_QROWS = 4
NA_KROWS = NA_QROWS + NA_WIN_ROWS - 1
MOE_BLOCK = 256
RT_TILE = 512
CB_TILE = 128
LC_CH = 8


def _cparams(sem):
    return pltpu.CompilerParams(dimension_semantics=sem, vmem_limit_bytes=VMEM_LIMIT)


def _mod_row(tile, tiles_per_batch, n_lat_tiles):
    return jnp.where(tile < n_lat_tiles, tile // tiles_per_batch, BATCH)


def _layer_norm(v, g, b):
    mu = jnp.mean(v, axis=-1, keepdims=True)
    c = v - mu
    var = jnp.mean(c * c, axis=-1, keepdims=True)
    return c * lax.rsqrt(var + LN_EPS) * g + b


def _ada_kernel(c_ref, w_ref, b_ref, o_ref):
    c = c_ref[...]
    cond = c * jax.nn.sigmoid(c)
    o_ref[0] = jnp.dot(cond, w_ref[0], precision=HIGHEST, preferred_element_type=F32) + b_ref[0]


def _ada(cc, w_ada, b_ada):
    depth, d, n = w_ada.shape
    tn = 1536
    return pl.pallas_call(
        _ada_kernel,
        grid=(depth, n // tn),
        in_specs=[
            pl.BlockSpec((16, d), lambda l, j: (0, 0)),
            pl.BlockSpec((1, d, tn), lambda l, j: (l, 0, j)),
            pl.BlockSpec((1, 1, tn), lambda l, j: (l, 0, j)),
        ],
        out_specs=pl.BlockSpec((1, 16, tn), lambda l, j: (l, 0, j)),
        out_shape=jax.ShapeDtypeStruct((depth, 16, n), F32),
        compiler_params=_cparams(("arbitrary", "arbitrary")),
        name="ada",
    )(cc, w_ada, b_ada.reshape(depth, 1, n))


def _proj_kernel(x_ref, sh_ref, sc_ref, w_hy, w_sc, w_q, w_k, w_v, w_g, o_hy, o_sc, o_q, o_k, o_v, o_g):
    h = (x_ref[...] * (1.0 + sc_ref[0]) + sh_ref[0]).astype(BF16)
    for w, o in ((w_hy, o_hy), (w_sc, o_sc), (w_q, o_q), (w_k, o_k), (w_v, o_v), (w_g, o_g)):
        o[...] = jnp.dot(h, w[...], preferred_element_type=F32).astype(o.dtype)


def _proj(x, shift, scale, ws, n_tok):
    tm = TOK_TILE
    nt = n_tok // tm
    mod = lambda t: (_mod_row(t, SEQ // tm, N_LAT // tm), 0, 0)
    widths = [w.shape[1] for w in ws]
    dtypes = [F32, F32, BF16, BF16, BF16, F32]
    return pl.pallas_call(
        _proj_kernel,
        grid=(nt,),
        in_specs=[
            pl.BlockSpec((tm, D_MODEL), lambda t: (t, 0)),
            pl.BlockSpec((1, 1, D_MODEL), mod),
            pl.BlockSpec((1, 1, D_MODEL), mod),
        ] + [pl.BlockSpec((D_MODEL, n), lambda t: (0, 0), pipeline_mode=pl.Buffered(1)) for n in widths],
        out_specs=[pl.BlockSpec((tm, n), lambda t: (t, 0)) for n in widths],
        out_shape=[jax.ShapeDtypeStruct((x.shape[0], n), dt) for n, dt in zip(widths, dtypes)],
        compiler_params=_cparams(("arbitrary",)),
        name="in_proj",
    )(x, shift, scale, *ws)


def _dwconv3(u, w):
    s = u.shape[0]
    row = lax.broadcasted_iota(I32, u.shape, 0)
    prev = jnp.where(row == 0, 0.0, pltpu.roll(u, 1, 0))
    nxt = jnp.where(row == s - 1, 0.0, pltpu.roll(u, s - 1, 0))
    return prev * w[0:1] + u * w[1:2] + nxt * w[2:3]


def _hy_pre_kernel(u0_ref, u1_ref, u2_ref, w0_ref, w1_ref, w2_ref, b0_ref, b1_ref, b2_ref, pz_ref, px_ref,
                   z_ref, x0_ref):
    del pz_ref, px_ref
    x0_ref[...] = _dwconv3(u0_ref[...], w0_ref[...]) + b0_ref[...]
    x1 = _dwconv3(u1_ref[...], w1_ref[...]) + b1_ref[...]
    v = _dwconv3(u2_ref[...], w2_ref[...]) + b2_ref[...]
    z_ref[...] = v * x1


def _hy_pre(u, conv_w, conv_b, prev, seq, row_off):
    ncb = D_HYENA // LANE
    ob = row_off // seq
    usp = lambda s: pl.BlockSpec((seq, LANE), lambda b, c: (ob + b, s * ncb + c))
    wsp = lambda s: pl.BlockSpec((3, LANE), lambda b, c: (0, s * ncb + c))
    bsp = lambda s: pl.BlockSpec((1, LANE), lambda b, c: (0, s * ncb + c))
    osp = pl.BlockSpec((seq, LANE), lambda b, c: (ob + b, c))
    n_tok = u.shape[0]
    if prev is None:
        prev = (jnp.zeros((n_tok, D_HYENA), F32), jnp.zeros((n_tok, D_HYENA), F32))
    args = [u, u, u, conv_w, conv_w, conv_w, conv_b, conv_b, conv_b] + list(prev)
    in_specs = [usp(0), usp(1), usp(2), wsp(0), wsp(1), wsp(2), bsp(0), bsp(1), bsp(2)]
    in_specs += [pl.BlockSpec(memory_space=pl.ANY)] * 2
    return pl.pallas_call(
        _hy_pre_kernel,
        grid=(BATCH, ncb),
        in_specs=in_specs,
        out_specs=[osp, osp],
        out_shape=[jax.ShapeDtypeStruct((n_tok, D_HYENA), F32)] * 2,
        input_output_aliases={9: 0, 10: 1},
        compiler_params=_cparams(("arbitrary", "arbitrary")),
        name="hyena_pre",
    )(*args)


def _sc_kernel(bg_ref, cg_ref, xs_ref, w_ref, prev_ref, o_ref):
    del prev_ref
    o_ref[...] = bg_ref[...] * _dwconv3(cg_ref[...] * xs_ref[...], w_ref[...])


def _short_conv(u, conv_w, prev, seq, row_off):
    ncb = D_SCONV // LANE
    ob = row_off // seq
    usp = lambda s: pl.BlockSpec((seq, LANE), lambda b, c: (ob + b, s * ncb + c))
    osp = pl.BlockSpec((seq, LANE), lambda b, c: (ob + b, c))
    if prev is None:
        prev = jnp.zeros((u.shape[0], D_SCONV), F32)
    in_specs = [usp(0), usp(1), usp(2), pl.BlockSpec((3, LANE), lambda b, c: (0, c)),
                pl.BlockSpec(memory_space=pl.ANY)]
    return pl.pallas_call(
        _sc_kernel,
        grid=(BATCH, ncb),
        in_specs=in_specs,
        out_specs=osp,
        out_shape=jax.ShapeDtypeStruct((u.shape[0], D_SCONV), F32),
        input_output_aliases={4: 0},
        compiler_params=_cparams(("arbitrary", "arbitrary")),
        name="short_conv",
    )(u, u, u, conv_w, prev)


def _filter_kernel(f_ref, dec_ref, w1_ref, b1_ref, w2_ref, b2_ref, w3_ref, fr_ref, o_ref):
    dot = functools.partial(jnp.dot, precision=HIGHEST, preferred_element_type=F32)
    z = jnp.sin(fr_ref[0, 0:1] * (dot(f_ref[...], w1_ref[0]) + b1_ref[0]))
    z = jnp.sin(fr_ref[0, 1:2] * (dot(z, w2_ref[0]) + b2_ref[0]))
    o_ref[0] = dot(z, w3_ref[0]) * dec_ref[...]


def _filter_consts(length):
    t = jnp.linspace(0.0, 1.0, length, dtype=F32)[:, None]
    ang = (2.0 * math.pi / length) * jnp.arange(length, dtype=F32)[:, None]
    bands = jnp.linspace(1e-4, HY_BANDS - 1, HY_BANDS, dtype=F32)[None, :]
    feats = jnp.concatenate([t, jnp.cos(bands * ang), -jnp.sin(bands * ang)], -1)
    feats = jnp.pad(feats, ((0, 0), (0, LANE - HY_EMB)))
    deltas = jnp.abs(jnp.linspace(math.log(HY_TARGET) / HY_SLOW_DECAY, math.log(HY_TARGET) / HY_FAST_DECAY,
                                  D_HYENA, dtype=F32))
    decay = jnp.exp(-t * deltas[None, :])
    return feats, jnp.concatenate([decay, decay], -1)


def _hyena_filters(length, w1, b1, w2, b2, w3, sin_freq):
    depth = w1.shape[0]
    pf = LANE - HY_FILTER_DIM
    w1p = jnp.pad(w1, ((0, 0), (0, LANE - HY_EMB), (0, pf)))
    b1p = jnp.pad(b1, ((0, 0), (0, pf)))[:, None, :]
    w2p = jnp.pad(w2, ((0, 0), (0, pf), (0, pf)))
    b2p = jnp.pad(b2, ((0, 0), (0, pf)))[:, None, :]
    w3p = jnp.pad(w3, ((0, 0), (0, pf), (0, 0)))
    frp = jnp.pad(sin_freq, ((0, 0), (0, 0), (0, pf)))
    feats, decay = _filter_consts(length)
    tl = min(length, 512)
    lsp = lambda shape: pl.BlockSpec((1,) + shape, lambda l, j: (l, 0, 0))
    h = pl.pallas_call(
        _filter_kernel,
        grid=(depth, length // tl),
        in_specs=[
            pl.BlockSpec((tl, LANE), lambda l, j: (j, 0)),
            pl.BlockSpec((tl, 2 * D_HYENA), lambda l, j: (j, 0)),
            lsp((LANE, LANE)), lsp((1, LANE)), lsp((LANE, LANE)), lsp((1, LANE)),
            lsp((LANE, 2 * D_HYENA)), lsp((2, LANE)),
        ],
        out_specs=pl.BlockSpec((1, tl, 2 * D_HYENA), lambda l, j: (l, j, 0)),
        out_shape=jax.ShapeDtypeStruct((depth, length, 2 * D_HYENA), F32),
        compiler_params=_cparams(("arbitrary", "arbitrary")),
        name="hyena_filter",
    )(feats, decay, w1p, b1p, w2p, b2p, w3p, frp)
    h_fwd = h[:, :, :D_HYENA]
    h_bwd = h[:, :, D_HYENA:]
    g_lin = jnp.concatenate([jnp.zeros((depth, 1, D_HYENA), F32), h_bwd[:, :0:-1], h_fwd], axis=1)
    return g_lin.transpose(0, 2, 1).reshape(depth, D_HYENA, 2 * length // LANE, LANE)


def _lconv_kernel(nb, z_ref, g_ref, y_ref, zl_ref):
    krow = lax.broadcasted_iota(I32, (LANE, LANE), 0)
    acol = lax.broadcasted_iota(I32, (LANE, LANE), 1)
    upper = acol >= krow

    def shifted(c, seg):
        return pltpu.roll(jnp.broadcast_to(g_ref[c, seg:seg + 1, :], (LANE, LANE)), 0, 1, stride=1, stride_axis=0)

    def per_channel(c, carry):
        for j in range(nb):
            zl_ref[j * BATCH:(j + 1) * BATCH, :] = z_ref[c, :, j * LANE:(j + 1) * LANE]
        y_ref[c] = jnp.zeros((nb * BATCH, LANE), F32)
        prev = shifted(c, 0)
        for d in range(1 - nb, nb):
            cur = shifted(c, d + nb)
            toep = jnp.where(upper, cur, prev).astype(BF16)
            j0, j1 = max(0, -d), min(nb, nb - d)
            part = jnp.dot(zl_ref[j0 * BATCH:j1 * BATCH, :].astype(BF16), toep, preferred_element_type=F32)
            y_ref[c, (j0 + d) * BATCH:(j1 + d) * BATCH, :] += part
            prev = cur
        return carry

    lax.fori_loop(0, LC_CH, per_channel, 0)


def _long_conv(zt, g):
    ch, _, length = zt.shape
    nb = length // LANE
    return pl.pallas_call(
        functools.partial(_lconv_kernel, nb),
        grid=(ch // LC_CH,),
        in_specs=[
            pl.BlockSpec((LC_CH, BATCH, length), lambda i: (i, 0, 0)),
            pl.BlockSpec((LC_CH, 2 * nb, LANE), lambda i: (i, 0, 0)),
        ],
        out_specs=pl.BlockSpec((LC_CH, nb * BATCH, LANE), lambda i: (i, 0, 0)),
        out_shape=jax.ShapeDtypeStruct((ch, nb * BATCH, LANE), F32),
        scratch_shapes=[pltpu.VMEM((nb * BATCH, LANE), F32)],
        compiler_params=_cparams(("arbitrary",)),
        name="hyena_long_conv",
    )(zt, g)


def _hyena_conv(z, g, seq, row_off):
    zt = z[row_off:row_off + BATCH * seq].reshape(BATCH, seq, D_HYENA).transpose(2, 0, 1)
    yt = _long_conv(zt, g)
    nb = seq // LANE
    return yt.reshape(D_HYENA, nb, BATCH, LANE).transpose(2, 1, 3, 0).reshape(BATCH * seq, D_HYENA)


def _na_bias_index():
    rows = SEQ // GRID_W
    ngrp = rows // NA_QROWS
    qcol = np.arange(GRID_W)
    cstart = np.clip(qcol - NA_WIN_COLS // 2, 0, GRID_W - NA_WIN_COLS)
    ridx = np.zeros((3, NA_QROWS, GRID_W, NA_KROWS, GRID_W), np.int32)
    rel = np.zeros_like(ridx)
    valid = np.zeros(ridx.shape, bool)
    for v, g in enumerate((0, 1, ngrp - 1)):
        u0 = int(np.clip(NA_QROWS * g - NA_WIN_ROWS // 2, 0, rows - NA_KROWS))
        for ri in range(NA_QROWS):
            r = NA_QROWS * g + ri
            rs = int(np.clip(r - NA_WIN_ROWS // 2, 0, rows - NA_WIN_ROWS))
            kr = u0 + np.arange(NA_KROWS)
            row_ok = (kr >= rs) & (kr < rs + NA_WIN_ROWS)
            col_ok = (qcol[None, :] >= cstart[:, None]) & (qcol[None, :] < cstart[:, None] + NA_WIN_COLS)
            ok = row_ok[None, :, None] & col_ok[:, None, :]
            valid[v, ri] = ok
            ridx[v, ri] = np.clip(kr - r + NA_WIN_ROWS - 1, 0, 2 * NA_WIN_ROWS - 2)[None, :, None]
            rel[v, ri] = np.clip(qcol[None, :] - qcol[:, None] + NA_WIN_COLS - 1, 0, 2 * NA_WIN_COLS - 2)[:, None, :]
    nq, nk = NA_QROWS * GRID_W, NA_KROWS * GRID_W
    return ridx.reshape(3, nq, nk), rel.reshape(3, nq, nk), valid.reshape(3, nq, nk)


def _na_bias(rpb):
    ridx, rel, valid = _na_bias_index()
    b = rpb[:, ridx, rel]
    return jnp.where(valid[None], b, NEG_INF).transpose(1, 0, 2, 3)


def _softmax_av(s_list, v_list):
    m = functools.reduce(jnp.maximum, [s.max(axis=-1, keepdims=True) for s in s_list])
    ps = [jnp.exp(s - m) for s in s_list]
    den = functools.reduce(jnp.add, [p.sum(axis=-1, keepdims=True) for p in ps])
    o = functools.reduce(jnp.add, [jnp.dot(p.astype(BF16), v, preferred_element_type=F32) for p, v in zip(ps, v_list)])
    return o / den


def _qk(q, k):
    return lax.dot_general(q, k, (((1,), (1,)), ((), ())), preferred_element_type=F32)


def _na_kernel(q_ref, k_ref, v_ref, kc_ref, vc_ref, bias_ref, prev_ref, o_ref):
    del prev_ref
    rows = SEQ // GRID_W
    g = pl.program_id(1)
    u0 = jnp.clip(NA_QROWS * g - NA_WIN_ROWS // 2, 0, rows - NA_KROWS)
    start = pl.multiple_of(u0 * GRID_W, GRID_W)
    nk = NA_KROWS * GRID_W
    scale = NA_HEAD_DIM ** -0.5
    for h in range(NA_HEADS):
        sl = slice(h * NA_HEAD_DIM, (h + 1) * NA_HEAD_DIM)
        qh = q_ref[:, sl]
        kh = k_ref[pl.ds(start, nk), sl]
        vh = v_ref[pl.ds(start, nk), sl]
        s_loc = _qk(qh, kh) * scale + bias_ref[0, h]
        s_ctx = _qk(qh, kc_ref[:, sl]) * scale
        o_ref[:, sl] = _softmax_av([s_loc, s_ctx], [vh, vc_ref[:, sl]]).astype(o_ref.dtype)


def _na_attention(q, k, v, bias):
    nq = NA_QROWS * GRID_W
    ngrp = SEQ // nq
    ctx0 = N_LAT // CTX_LEN

    def variant(b, g):
        return (jnp.where(g == 0, 0, jnp.where(g == ngrp - 1, 2, 1)), 0, 0, 0)

    return pl.pallas_call(
        _na_kernel,
        grid=(BATCH, ngrp),
        in_specs=[
            pl.BlockSpec((nq, D_NA), lambda b, g: (b * ngrp + g, 0)),
            pl.BlockSpec((SEQ, D_NA), lambda b, g: (b, 0)),
            pl.BlockSpec((SEQ, D_NA), lambda b, g: (b, 0)),
            pl.BlockSpec((CTX_LEN, D_NA), lambda b, g: (ctx0 + b, 0)),
            pl.BlockSpec((CTX_LEN, D_NA), lambda b, g: (ctx0 + b, 0)),
            pl.BlockSpec((1, NA_HEADS, nq, NA_KROWS * GRID_W), variant),
            pl.BlockSpec(memory_space=pl.ANY),
        ],
        out_specs=pl.BlockSpec((nq, D_NA), lambda b, g: (b * ngrp + g, 0)),
        out_shape=jax.ShapeDtypeStruct((N_TOK, D_NA), BF16),
        input_output_aliases={6: 0},
        compiler_params=_cparams(("arbitrary", "arbitrary")),
        name="na_attention",
    )(q, k, v, k, v, bias, jnp.zeros((N_TOK, D_NA), BF16))


def _ctx_attn_kernel(q_ref, k_ref, v_ref, prev_ref, o_ref):
    del prev_ref
    scale = NA_HEAD_DIM ** -0.5
    for h in range(NA_HEADS):
        sl = slice(h * NA_HEAD_DIM, (h + 1) * NA_HEAD_DIM)
        s = _qk(q_ref[:, sl], k_ref[:, sl]) * scale
        o_ref[:, sl] = _softmax_av([s], [v_ref[:, sl]]).astype(o_ref.dtype)


def _ctx_attention(q, k, v, att):
    ctx0 = N_LAT // CTX_LEN
    sp = pl.BlockSpec((CTX_LEN, D_NA), lambda b: (ctx0 + b, 0))
    return pl.pallas_call(
        _ctx_attn_kernel,
        grid=(BATCH,),
        in_specs=[sp, sp, sp, pl.BlockSpec(memory_space=pl.ANY)],
        out_specs=sp,
        out_shape=jax.ShapeDtypeStruct(att.shape, att.dtype),
        input_output_aliases={3: 0},
        compiler_params=_cparams(("arbitrary",)),
        name="ctx_attention",
    )(q, k, v, att)


def _mix_kernel(yc_ref, z_ref, x0_ref, ysc_ref, ya_ref, gt_ref, xl_ref, g1_ref, sh2_ref, sc2_ref, bd_ref,
                wh_ref, ws_ref, wn_ref, wo_ref, lg_ref, lb_ref, xo_ref, ho_ref):
    dot = functools.partial(jnp.dot, preferred_element_type=F32)
    y_hy = x0_ref[...] * (yc_ref[...] + z_ref[...] * bd_ref[...])
    merged = (jax.nn.sigmoid(gt_ref[:, 0:D_MODEL]) * dot(y_hy.astype(BF16), wh_ref[...])
              + jax.nn.sigmoid(gt_ref[:, D_MODEL:2 * D_MODEL]) * dot(ysc_ref[...].astype(BF16), ws_ref[...])
              + jax.nn.sigmoid(gt_ref[:, 2 * D_MODEL:3 * D_MODEL]) * dot(ya_ref[...], wn_ref[...]))
    out = dot(merged.astype(BF16), wo_ref[...])
    xo = _layer_norm(ALPHA * xl_ref[...] + g1_ref[0] * out, lg_ref[...], lb_ref[...])
    xo_ref[...] = xo
    ho_ref[...] = xo * (1.0 + sc2_ref[0]) + sh2_ref[0]


def _mix(yc, z, x0, ysc, ya, gates, xl, g1, sh2, sc2, bias_d, wh, ws, wn, wo, ln_g, ln_b, n_tok):
    tm = TOK_TILE
    tok = lambda n: pl.BlockSpec((tm, n), lambda t: (t, 0))
    mod = pl.BlockSpec((1, 1, D_MODEL), lambda t: (_mod_row(t, SEQ // tm, N_LAT // tm), 0, 0))
    const = lambda r, n: pl.BlockSpec((r, n), lambda t: (0, 0))
    return pl.pallas_call(
        _mix_kernel,
        grid=(n_tok // tm,),
        in_specs=[tok(D_HYENA), tok(D_HYENA), tok(D_HYENA), tok(D_SCONV), tok(D_NA), tok(3 * D_MODEL), tok(D_MODEL),
                  mod, mod, mod, const(1, D_HYENA),
                  const(D_HYENA, D_MODEL), const(D_SCONV, D_MODEL), const(D_NA, D_MODEL), const(D_MODEL, D_MODEL),
                  const(1, D_MODEL), const(1, D_MODEL)],
        out_specs=[tok(D_MODEL), tok(D_MODEL)],
        out_shape=[jax.ShapeDtypeStruct((xl.shape[0], D_MODEL), F32)] * 2,
        compiler_params=_cparams(("arbitrary",)),
        name="mixer_out",
    )(yc, z, x0, ysc, ya, gates, xl, g1, sh2, sc2, bias_d, wh, ws, wn, wo, ln_g, ln_b)


def _router_kernel(h_ref, wr_ref, rb_ref, tri_ref, te_ref, rk_ref, w_ref, cnt_ref, run_ref):
    tm = h_ref.shape[0]
    per = N_EXPERTS // N_GROUPS

    @pl.when(pl.program_id(0) == 0)
    def _():
        run_ref[...] = jnp.zeros_like(run_ref)

    logits = lax.dot_general(wr_ref[...], h_ref[...], (((1,), (1,)), ((), ())),
                             precision=HIGHEST, preferred_element_type=F32)
    scores = jax.nn.sigmoid(logits)
    sel = scores + rb_ref[...]
    sub = lax.broadcasted_iota(I32, (per, tm), 0)
    colmax = lambda a: jnp.max(a, axis=0, keepdims=True)
    colmin = lambda a: jnp.min(a, axis=0, keepdims=True)
    ninf = -jnp.inf

    xs = [sel[g * per:(g + 1) * per, :] for g in range(N_GROUPS)]
    sc = [scores[g * per:(g + 1) * per, :] for g in range(N_GROUPS)]
    gs = []
    for x in xs:
        m1 = colmax(x)
        i1 = colmin(jnp.where(x == m1, sub, per))
        m2 = colmax(jnp.where(sub == i1, ninf, x))
        gs.append(m1 + m2)
    chosen = [jnp.zeros((1, tm), F32) for _ in range(N_GROUPS)]
    for _ in range(TOPK_GROUPS):
        gm = functools.reduce(jnp.maximum, gs)
        gi = jnp.full((1, tm), N_GROUPS, I32)
        for g in reversed(range(N_GROUPS)):
            gi = jnp.where(gs[g] == gm, g, gi)
        for g in range(N_GROUPS):
            hit = gi == g
            chosen[g] = jnp.where(hit, 1.0, chosen[g])
            gs[g] = jnp.where(hit, ninf, gs[g])
    xm = [jnp.where(jnp.broadcast_to(chosen[g], (per, tm)) > 0.5, xs[g], ninf) for g in range(N_GROUPS)]
    eidx = [sub + g * per for g in range(N_GROUPS)]
    picked = [jnp.zeros((per, tm), F32) for _ in range(N_GROUPS)]
    top_e, top_s = [], []
    for _ in range(TOP_K):
        em = functools.reduce(jnp.maximum, [colmax(x) for x in xm])
        ei = functools.reduce(jnp.minimum, [colmin(jnp.where(xm[g] == em, eidx[g], N_EXPERTS)) for g in range(N_GROUPS)])
        s_acc = jnp.zeros((1, tm), F32)
        for g in range(N_GROUPS):
            hit = eidx[g] == ei
            picked[g] = jnp.where(hit, 1.0, picked[g])
            xm[g] = jnp.where(hit, ninf, xm[g])
            s_acc = s_acc + jnp.sum(jnp.where(hit, sc[g], 0.0), axis=0, keepdims=True)
        top_e.append(ei)
        top_s.append(s_acc)
    den = functools.reduce(jnp.add, top_s)
    mask = jnp.concatenate(picked, axis=0)
    prefix = jnp.dot(mask.astype(BF16), tri_ref[...], preferred_element_type=F32)
    pos = run_ref[:, 0:1] + prefix
    posg = [pos[g * per:(g + 1) * per, :] for g in range(N_GROUPS)]
    for i in range(TOP_K):
        r_acc = jnp.zeros((1, tm), F32)
        for g in range(N_GROUPS):
            r_acc = r_acc + jnp.sum(jnp.where(eidx[g] == top_e[i], posg[g], 0.0), axis=0, keepdims=True)
        te_ref[0, i:i + 1, :] = top_e[i]
        rk_ref[0, i:i + 1, :] = r_acc.astype(I32)
        w_ref[0, i:i + 1, :] = top_s[i] / den * ROUTED_SCALE
    run_ref[...] = run_ref[...] + jnp.sum(mask, axis=1, keepdims=True)
    cnt_ref[...] = run_ref[...]


def _route(tokens, router_w_t, router_b, n_tok):
    tm = RT_TILE
    nt = n_tok // tm
    tri = jnp.asarray(np.triu(np.ones((tm, tm), np.float32), 1), BF16)
    out3 = lambda dt: jax.ShapeDtypeStruct((nt, TOP_K, tm), dt)
    osp = pl.BlockSpec((1, TOP_K, tm), lambda t: (t, 0, 0))
    return pl.pallas_call(
        _router_kernel,
        grid=(nt,),
        in_specs=[
            pl.BlockSpec((tm, D_MODEL), lambda t: (t, 0)),
            pl.BlockSpec((N_EXPERTS, D_MODEL), lambda t: (0, 0)),
            pl.BlockSpec((N_EXPERTS, 1), lambda t: (0, 0)),
            pl.BlockSpec((tm, tm), lambda t: (0, 0)),
        ],
        out_specs=[osp, osp, osp, pl.BlockSpec((N_EXPERTS, LANE), lambda t: (0, 0))],
        out_shape=[out3(I32), out3(I32), out3(F32), jax.ShapeDtypeStruct((N_EXPERTS, LANE), F32)],
        scratch_shapes=[pltpu.VMEM((N_EXPERTS, LANE), F32)],
        compiler_params=_cparams(("arbitrary",)),
        name="moe_router",
    )(tokens, router_w_t, router_b, tri)


def _dispatch_kernel(cnt_ref, pst_ref, na_ref, dest_ref, h_hbm, xs_hbm, zero_ref, sem):
    tm = dest_ref.shape[2]
    t0 = pl.program_id(0) * tm
    row_copy = lambda src, dst: pltpu.make_async_copy(src, xs_hbm.at[pl.ds(dst, 1)], sem)
    n_blocks = xs_hbm.shape[0] // MOE_BLOCK

    @pl.when(pl.program_id(0) == 0)
    def _():
        zero_ref[...] = jnp.zeros_like(zero_ref)

        def blk_copy(blk):
            return pltpu.make_async_copy(zero_ref, xs_hbm.at[pl.ds(pl.multiple_of(blk * MOE_BLOCK, MOE_BLOCK), MOE_BLOCK)], sem)

        def fill_blk(blk, c):
            blk_copy(blk).start()
            return c

        def drain_blk(blk, c):
            blk_copy(blk).wait()
            return c

        lax.fori_loop(na_ref[0], n_blocks, fill_blk, 0)
        lax.fori_loop(na_ref[0], n_blocks, drain_blk, 0)

        def per_expert(e, carry):
            n = cnt_ref[e]
            n_pad = (n + MOE_BLOCK - 1) // MOE_BLOCK * MOE_BLOCK - n
            base = pst_ref[e] + n

            def fill(r, c):
                row_copy(zero_ref.at[pl.ds(0, 1)], base + r).start()
                return c

            def drain(r, c):
                row_copy(zero_ref.at[pl.ds(0, 1)], base).wait()
                return c

            lax.fori_loop(0, n_pad, fill, 0)
            lax.fori_loop(0, n_pad, drain, 0)
            return carry

        lax.fori_loop(0, N_EXPERTS, per_expert, 0)

    def issue(t, carry):
        for i in range(TOP_K):
            row_copy(h_hbm.at[pl.ds(t0 + t, 1)], dest_ref[0, i, t]).start()
        return carry

    def drain(t, carry):
        for i in range(TOP_K):
            row_copy(h_hbm.at[pl.ds(0, 1)], 0).wait()
        return carry

    lax.fori_loop(0, tm, issue, 0)
    lax.fori_loop(0, tm, drain, 0)


def _dispatch(counts, pstart, n_active, dest, tokens, n_slots):
    nt, _, tm = dest.shape
    grid_spec = pltpu.PrefetchScalarGridSpec(
        num_scalar_prefetch=3,
        grid=(nt,),
        in_specs=[
            pl.BlockSpec((1, TOP_K, tm), lambda t, c, p, a: (t, 0, 0), memory_space=pltpu.SMEM),
            pl.BlockSpec(memory_space=pl.ANY),
        ],
        out_specs=pl.BlockSpec(memory_space=pl.ANY),
        scratch_shapes=[pltpu.VMEM((MOE_BLOCK, D_MODEL), F32), pltpu.SemaphoreType.DMA],
    )
    return pl.pallas_call(
        _dispatch_kernel,
        grid_spec=grid_spec,
        out_shape=jax.ShapeDtypeStruct((n_slots, D_MODEL), F32),
        compiler_params=_cparams(("arbitrary",)),
        name="moe_dispatch",
    )(counts, pstart, n_active, dest, tokens)


def _expert_kernel(be_ref, na_ref, x_ref, w1_ref, w3_ref, w2_ref, y_ref):
    active = pl.program_id(0) < na_ref[0]

    @pl.when(active)
    def _():
        dot = functools.partial(jnp.dot, preferred_element_type=F32)
        x = x_ref[...].astype(BF16)
        a = dot(x, w1_ref[0])
        hidden = (a * jax.nn.sigmoid(a)) * dot(x, w3_ref[0])
        y_ref[...] = dot(hidden.astype(BF16), w2_ref[0])

    @pl.when(jnp.logical_not(active))
    def _():
        y_ref[...] = jnp.zeros_like(y_ref)


def _experts(blk_expert, n_active, xs, w1, w3, w2):
    n_blocks = xs.shape[0] // MOE_BLOCK
    xmap = lambda i, be, na: (jnp.minimum(i, na[0] - 1), 0)
    wmap = lambda i, be, na: (be[i], 0, 0)
    grid_spec = pltpu.PrefetchScalarGridSpec(
        num_scalar_prefetch=2,
        grid=(n_blocks,),
        in_specs=[
            pl.BlockSpec((MOE_BLOCK, D_MODEL), xmap),
            pl.BlockSpec((1, D_MODEL, D_EXPERT), wmap),
            pl.BlockSpec((1, D_MODEL, D_EXPERT), wmap),
            pl.BlockSpec((1, D_EXPERT, D_MODEL), wmap),
        ],
        out_specs=pl.BlockSpec((MOE_BLOCK, D_MODEL), lambda i, be, na: (i, 0)),
    )
    return pl.pallas_call(
        _expert_kernel,
        grid_spec=grid_spec,
        out_shape=jax.ShapeDtypeStruct(xs.shape, F32),
        compiler_params=_cparams(("arbitrary",)),
        name="moe_experts",
    )(blk_expert, n_active, xs, w1, w3, w2)


def _combine_kernel(dest_ref, w_ref, h_ref, xl_ref, g2_ref, s1_ref, s3_ref, s2_ref, lg_ref, lb_ref, ys_hbm,
                    xo_ref, ybuf, sem):
    tm = h_ref.shape[0]

    def row_copy(i, t, src):
        return pltpu.make_async_copy(ys_hbm.at[pl.ds(src, 1)], ybuf.at[i, pl.ds(t, 1)], sem)

    def issue(t, carry):
        for i in range(TOP_K):
            row_copy(i, t, dest_ref[0, i, t]).start()
        return carry

    def drain(t, carry):
        for i in range(TOP_K):
            row_copy(i, t, 0).wait()
        return carry

    lax.fori_loop(0, tm, issue, 0)
    dot = functools.partial(jnp.dot, preferred_element_type=F32)
    hb = h_ref[...].astype(BF16)
    a = dot(hb, s1_ref[...])
    f = dot(((a * jax.nn.sigmoid(a)) * dot(hb, s3_ref[...])).astype(BF16), s2_ref[...])
    lax.fori_loop(0, tm, drain, 0)
    for i in range(TOP_K):
        f = f + w_ref[:, i:i + 1] * ybuf[i]
    xo_ref[...] = _layer_norm(ALPHA * xl_ref[...] + g2_ref[0] * f, lg_ref[...], lb_ref[...])


def _combine(dest, w, h, xl, g2, s1, s3, s2, ln_g, ln_b, ys, n_tok):
    tm = CB_TILE
    nt = n_tok // tm
    tok = lambda n: pl.BlockSpec((tm, n), lambda t: (t, 0))
    const = lambda r, n: pl.BlockSpec((r, n), lambda t: (0, 0))
    return pl.pallas_call(
        _combine_kernel,
        grid=(nt,),
        in_specs=[
            pl.BlockSpec((1, TOP_K, tm), lambda t: (t, 0, 0), memory_space=pltpu.SMEM),
            tok(TOP_K), tok(D_MODEL), tok(D_MODEL),
            pl.BlockSpec((1, 1, D_MODEL), lambda t: (_mod_row(t, SEQ // tm, N_LAT // tm), 0, 0)),
            const(D_MODEL, D_SHARED), const(D_MODEL, D_SHARED), const(D_SHARED, D_MODEL),
            const(1, D_MODEL), const(1, D_MODEL),
            pl.BlockSpec(memory_space=pl.ANY),
        ],
        out_specs=tok(D_MODEL),
        out_shape=jax.ShapeDtypeStruct((n_tok, D_MODEL), F32),
        scratch_shapes=[pltpu.VMEM((TOP_K, tm, D_MODEL), F32), pltpu.SemaphoreType.DMA],
        compiler_params=_cparams(("arbitrary",)),
        name="moe_combine",
    )(dest, w, h, xl, g2, s1, s3, s2, ln_g, ln_b, ys)


def _moe_layer(h, xl, g2, router_w_t, router_b, w1, w3, w2, s1, s3, s2, ln_g, ln_b, n_tok):
    top_e, rank, wts, cnt = _route(h, router_w_t, router_b, n_tok)
    counts = cnt[:, 0].astype(I32)
    padded = (counts + MOE_BLOCK - 1) // MOE_BLOCK * MOE_BLOCK
    pend = jnp.cumsum(padded)
    pstart = pend - padded
    n_blocks = n_tok * TOP_K // MOE_BLOCK + N_EXPERTS
    onehot = top_e[..., None] == jnp.arange(N_EXPERTS, dtype=I32)
    dest = rank + jnp.sum(jnp.where(onehot, pstart, 0), axis=-1)
    blk_expert = jnp.minimum(
        jnp.searchsorted(pend // MOE_BLOCK, jnp.arange(n_blocks, dtype=I32), side="right"), N_EXPERTS - 1).astype(I32)
    n_active = (pend[-1:] // MOE_BLOCK).astype(I32)
    xs = _dispatch(counts, pstart.astype(I32), n_active, dest, h, n_blocks * MOE_BLOCK)
    ys = _experts(blk_expert, n_active, xs, w1, w3, w2)
    nt = n_tok // CB_TILE
    regroup = lambda a: a.transpose(1, 0, 2).reshape(TOP_K, n_tok)
    dest_c = regroup(dest).reshape(TOP_K, nt, CB_TILE).transpose(1, 0, 2)
    w_c = regroup(wts).T
    return _combine(dest_c, w_c, h, xl, g2, s1, s3, s2, ln_g, ln_b, ys, n_tok)


def kernel(x, c, ctx, c_ctx, w_ada, b_ada, w_in, hy_conv_w, hy_conv_b, hy_w1, hy_b1, hy_w2, hy_b2, hy_w3, hy_sin_freq, hy_bias_d, hy_proj, sc_conv_w, sc_proj, na_rpb, na_proj, w_o, ln1_g, ln1_b, ln2_g, ln2_b, moe_router, moe_bias, moe_w1, moe_w3, moe_w2, sh_w1, sh_w3, sh_w2):
    bf = lambda a: a.astype(BF16)
    xa = jnp.concatenate([x.reshape(N_LAT, D_MODEL), ctx.reshape(N_CTX, D_MODEL)], axis=0)

    cc = jnp.zeros((16, D_MODEL), F32).at[:BATCH].set(c).at[BATCH].set(c_ctx)
    mods = _ada(cc, w_ada, b_ada)[:, :BATCH + 1].reshape(DEPTH, BATCH + 1, 6, 1, D_MODEL)
    filt_lat = _hyena_filters(SEQ, hy_w1, hy_b1, hy_w2, hy_b2, hy_w3, hy_sin_freq)
    filt_ctx = _hyena_filters(CTX_LEN, hy_w1, hy_b1, hy_w2, hy_b2, hy_w3, hy_sin_freq)

    offs = [0, 3 * D_HYENA, 3 * D_HYENA + 3 * D_SCONV]
    offs += [offs[2] + D_NA, offs[2] + 2 * D_NA, offs[2] + 3 * D_NA, w_in.shape[2]]

    for i in range(DEPTH):
        last = i == DEPTH - 1
        m = [mods[i, :, j] for j in range(6)]
        w_secs = [bf(w_in[i][:, offs[j]:offs[j + 1]]) for j in range(6)]
        u_hy, u_sc, q, k, v, gates = _proj(xa, m[0], m[1], w_secs, N_TOK)

        z, x0 = _hy_pre(u_hy, hy_conv_w[i], hy_conv_b[i][None], None, SEQ, 0)
        z, x0 = _hy_pre(u_hy, hy_conv_w[i], hy_conv_b[i][None], (z, x0), CTX_LEN, N_LAT)
        yc = jnp.concatenate([_hyena_conv(z, filt_lat[i], SEQ, 0), _hyena_conv(z, filt_ctx[i], CTX_LEN, N_LAT)], axis=0)
        ysc = _short_conv(u_sc, sc_conv_w[i], None, SEQ, 0)
        ysc = _short_conv(u_sc, sc_conv_w[i], ysc, CTX_LEN, N_LAT)
        att = _na_attention(q, k, v, _na_bias(na_rpb[i]))
        att = _ctx_attention(q, k, v, att)

        xa, hmoe = _mix(yc, z, x0, ysc, att, gates, xa, m[2], m[3], m[4], hy_bias_d[i][None],
                        bf(hy_proj[i]), bf(sc_proj[i]), bf(na_proj[i]), bf(w_o[i]), ln1_g[i][None], ln1_b[i][None], N_TOK)

        n_moe = N_LAT if last else N_TOK
        xa = _moe_layer(hmoe, xa, m[5], moe_router[i].T, moe_bias[i][:, None], bf(moe_w1[i]), bf(moe_w3[i]),
                        bf(moe_w2[i]), bf(sh_w1[i]), bf(sh_w3[i]), bf(sh_w2[i]), ln2_g[i][None], ln2_b[i][None], n_moe)
    return xa.reshape(BATCH, SEQ, D_MODEL)
```

```python
import functools
import math

import numpy as np
import jax
import jax.numpy as jnp
from jax import lax
from jax.experimental import pallas as pl
from jax.experimental.pallas import tpu as pltpu

F32 = jnp.float32
BF16 = jnp.bfloat16
I32 = jnp.int32
HIGHEST = lax.Precision.HIGHEST

D_MODEL = 1024
BATCH = 8
SEQ = 4096
DEPTH = 4
CTX_LEN = 256
GRID_W = 64
D_HYENA = 512
D_SCONV = 512
NA_HEADS = 8
NA_HEAD_DIM = 64
D_NA = NA_HEADS * NA_HEAD_DIM
NA_WIN_ROWS = 8
NA_WIN_COLS = 16
HY_BANDS = 16
HY_EMB = 1 + 2 * HY_BANDS
HY_FILTER_DIM = 64
HY_FAST_DECAY = 0.3
HY_SLOW_DECAY = 1.5
HY_TARGET = 1e-2
N_EXPERTS = 64
N_GROUPS = 8
TOPK_GROUPS = 4
TOP_K = 8
D_EXPERT = 256
D_SHARED = 256
ROUTED_SCALE = 2.5
LN_EPS = 1e-5
NEG_INF = -1e30
ALPHA = (2.0 * DEPTH) ** 0.25

N_LAT = BATCH * SEQ
N_CTX = BATCH * CTX_LEN
N_TOK = N_LAT + N_CTX

LANE = 128
SUBLANE = 8
VMEM_LIMIT = 56 * 1024 * 1024

TOK_TILE = 256
NA_QROWS = 4
NA_KROWS = NA_QROWS + NA_WIN_ROWS - 1
MOE_BLOCK = 256
RT_TILE = 512
CB_TILE = 128
LC_CH = 8


def _cparams(sem):
    return pltpu.CompilerParams(dimension_semantics=sem, vmem_limit_bytes=VMEM_LIMIT)


def _mod_row(tile, tiles_per_batch, n_lat_tiles):
    return jnp.where(tile < n_lat_tiles, tile // tiles_per_batch, BATCH)


def _layer_norm(v, g, b):
    mu = jnp.mean(v, axis=-1, keepdims=True)
    c = v - mu
    var = jnp.mean(c * c, axis=-1, keepdims=True)
    return c * lax.rsqrt(var + LN_EPS) * g + b


def _ada_kernel(c_ref, w_ref, b_ref, o_ref):
    c = c_ref[...]
    cond = c * jax.nn.sigmoid(c)
    o_ref[0] = jnp.dot(cond, w_ref[0], precision=HIGHEST, preferred_element_type=F32) + b_ref[0]


def _ada(cc, w_ada, b_ada):
    depth, d, n = w_ada.shape
    tn = 1536
    return pl.pallas_call(
        _ada_kernel,
        grid=(depth, n // tn),
        in_specs=[
            pl.BlockSpec((16, d), lambda l, j: (0, 0)),
            pl.BlockSpec((1, d, tn), lambda l, j: (l, 0, j)),
            pl.BlockSpec((1, 1, tn), lambda l, j: (l, 0, j)),
        ],
        out_specs=pl.BlockSpec((1, 16, tn), lambda l, j: (l, 0, j)),
        out_shape=jax.ShapeDtypeStruct((depth, 16, n), F32),
        compiler_params=_cparams(("arbitrary", "arbitrary")),
        name="ada",
    )(cc, w_ada, b_ada.reshape(depth, 1, n))


def _proj_kernel(x_ref, sh_ref, sc_ref, w_hy, w_sc, w_q, w_k, w_v, w_g, o_hy, o_sc, o_q, o_k, o_v, o_g):
    h = (x_ref[...] * (1.0 + sc_ref[0]) + sh_ref[0]).astype(BF16)
    for w, o in ((w_hy, o_hy), (w_sc, o_sc), (w_q, o_q), (w_k, o_k), (w_v, o_v), (w_g, o_g)):
        o[...] = jnp.dot(h, w[...], preferred_element_type=F32).astype(o.dtype)


def _proj(x, shift, scale, ws, n_tok):
    tm = TOK_TILE
    nt = n_tok // tm
    mod = lambda t: (_mod_row(t, SEQ // tm, N_LAT // tm), 0, 0)
    widths = [w.shape[1] for w in ws]
    dtypes = [F32, F32, BF16, BF16, BF16, F32]
    return pl.pallas_call(
        _proj_kernel,
        grid=(nt,),
        in_specs=[
            pl.BlockSpec((tm, D_MODEL), lambda t: (t, 0)),
            pl.BlockSpec((1, 1, D_MODEL), mod),
            pl.BlockSpec((1, 1, D_MODEL), mod),
        ] + [pl.BlockSpec((D_MODEL, n), lambda t: (0, 0), pipeline_mode=pl.Buffered(1)) for n in widths],
        out_specs=[pl.BlockSpec((tm, n), lambda t: (t, 0)) for n in widths],
        out_shape=[jax.ShapeDtypeStruct((x.shape[0], n), dt) for n, dt in zip(widths, dtypes)],
        compiler_params=_cparams(("arbitrary",)),
        name="in_proj",
    )(x, shift, scale, *ws)


def _dwconv3(u, w):
    s = u.shape[0]
    row = lax.broadcasted_iota(I32, u.shape, 0)
    prev = jnp.where(row == 0, 0.0, pltpu.roll(u, 1, 0))
    nxt = jnp.where(row == s - 1, 0.0, pltpu.roll(u, s - 1, 0))
    return prev * w[0:1] + u * w[1:2] + nxt * w[2:3]


def _hy_pre_kernel(u0_ref, u1_ref, u2_ref, w0_ref, w1_ref, w2_ref, b0_ref, b1_ref, b2_ref, pz_ref, px_ref,
                   z_ref, x0_ref):
    del pz_ref, px_ref
    x0_ref[...] = _dwconv3(u0_ref[...], w0_ref[...]) + b0_ref[...]
    x1 = _dwconv3(u1_ref[...], w1_ref[...]) + b1_ref[...]
    v = _dwconv3(u2_ref[...], w2_ref[...]) + b2_ref[...]
    z_ref[...] = v * x1


def _hy_pre(u, conv_w, conv_b, prev, seq, row_off):
    ncb = D_HYENA // LANE
    ob = row_off // seq
    usp = lambda s: pl.BlockSpec((seq, LANE), lambda b, c: (ob + b, s * ncb + c))
    wsp = lambda s: pl.BlockSpec((3, LANE), lambda b, c: (0, s * ncb + c))
    bsp = lambda s: pl.BlockSpec((1, LANE), lambda b, c: (0, s * ncb + c))
    osp = pl.BlockSpec((seq, LANE), lambda b, c: (ob + b, c))
    n_tok = u.shape[0]
    if prev is None:
        prev = (jnp.zeros((n_tok, D_HYENA), F32), jnp.zeros((n_tok, D_HYENA), F32))
    args = [u, u, u, conv_w, conv_w, conv_w, conv_b, conv_b, conv_b] + list(prev)
    in_specs = [usp(0), usp(1), usp(2), wsp(0), wsp(1), wsp(2), bsp(0), bsp(1), bsp(2)]
    in_specs += [pl.BlockSpec(memory_space=pl.ANY)] * 2
    return pl.pallas_call(
        _hy_pre_kernel,
        grid=(BATCH, ncb),
        in_specs=in_specs,
        out_specs=[osp, osp],
        out_shape=[jax.ShapeDtypeStruct((n_tok, D_HYENA), F32)] * 2,
        input_output_aliases={9: 0, 10: 1},
        compiler_params=_cparams(("arbitrary", "arbitrary")),
        name="hyena_pre",
    )(*args)


def _sc_kernel(bg_ref, cg_ref, xs_ref, w_ref, prev_ref, o_ref):
    del prev_ref
    o_ref[...] = bg_ref[...] * _dwconv3(cg_ref[...] * xs_ref[...], w_ref[...])


def _short_conv(u, conv_w, prev, seq, row_off):
    ncb = D_SCONV // LANE
    ob = row_off // seq
    usp = lambda s: pl.BlockSpec((seq, LANE), lambda b, c: (ob + b, s * ncb + c))
    osp = pl.BlockSpec((seq, LANE), lambda b, c: (ob + b, c))
    if prev is None:
        prev = jnp.zeros((u.shape[0], D_SCONV), F32)
    in_specs = [usp(0), usp(1), usp(2), pl.BlockSpec((3, LANE), lambda b, c: (0, c)),
                pl.BlockSpec(memory_space=pl.ANY)]
    return pl.pallas_call(
        _sc_kernel,
        grid=(BATCH, ncb),
        in_specs=in_specs,
        out_specs=osp,
        out_shape=jax.ShapeDtypeStruct((u.shape[0], D_SCONV), F32),
        input_output_aliases={4: 0},
        compiler_params=_cparams(("arbitrary", "arbitrary")),
        name="short_conv",
    )(u, u, u, conv_w, prev)


def _filter_kernel(f_ref, dec_ref, w1_ref, b1_ref, w2_ref, b2_ref, w3_ref, fr_ref, o_ref):
    dot = functools.partial(jnp.dot, precision=HIGHEST, preferred_element_type=F32)
    z = jnp.sin(fr_ref[0, 0:1] * (dot(f_ref[...], w1_ref[0]) + b1_ref[0]))
    z = jnp.sin(fr_ref[0, 1:2] * (dot(z, w2_ref[0]) + b2_ref[0]))
    o_ref[0] = dot(z, w3_ref[0]) * dec_ref[...]


def _filter_consts(length):
    t = jnp.linspace(0.0, 1.0, length, dtype=F32)[:, None]
    ang = (2.0 * math.pi / length) * jnp.arange(length, dtype=F32)[:, None]
    bands = jnp.linspace(1e-4, HY_BANDS - 1, HY_BANDS, dtype=F32)[None, :]
    feats = jnp.concatenate([t, jnp.cos(bands * ang), -jnp.sin(bands * ang)], -1)
    feats = jnp.pad(feats, ((0, 0), (0, LANE - HY_EMB)))
    deltas = jnp.abs(jnp.linspace(math.log(HY_TARGET) / HY_SLOW_DECAY, math.log(HY_TARGET) / HY_FAST_DECAY,
                                  D_HYENA, dtype=F32))
    decay = jnp.exp(-t * deltas[None, :])
    return feats, jnp.concatenate([decay, decay], -1)


def _hyena_filters(length, w1, b1, w2, b2, w3, sin_freq):
    depth = w1.shape[0]
    pf = LANE - HY_FILTER_DIM
    w1p = jnp.pad(w1, ((0, 0), (0, LANE - HY_EMB), (0, pf)))
    b1p = jnp.pad(b1, ((0, 0), (0, pf)))[:, None, :]
    w2p = jnp.pad(w2, ((0, 0), (0, pf), (0, pf)))
    b2p = jnp.pad(b2, ((0, 0), (0, pf)))[:, None, :]
    w3p = jnp.pad(w3, ((0, 0), (0, pf), (0, 0)))
    frp = jnp.pad(sin_freq, ((0, 0), (0, 0), (0, pf)))
    feats, decay = _filter_consts(length)
    tl = min(length, 512)
    lsp = lambda shape: pl.BlockSpec((1,) + shape, lambda l, j: (l, 0, 0))
    h = pl.pallas_call(
        _filter_kernel,
        grid=(depth, length // tl),
        in_specs=[
            pl.BlockSpec((tl, LANE), lambda l, j: (j, 0)),
            pl.BlockSpec((tl, 2 * D_HYENA), lambda l, j: (j, 0)),
            lsp((LANE, LANE)), lsp((1, LANE)), lsp((LANE, LANE)), lsp((1, LANE)),
            lsp((LANE, 2 * D_HYENA)), lsp((2, LANE)),
        ],
        out_specs=pl.BlockSpec((1, tl, 2 * D_HYENA), lambda l, j: (l, j, 0)),
        out_shape=jax.ShapeDtypeStruct((depth, length, 2 * D_HYENA), F32),
        compiler_params=_cparams(("arbitrary", "arbitrary")),
        name="hyena_filter",
    )(feats, decay, w1p, b1p, w2p, b2p, w3p, frp)
    h_fwd = h[:, :, :D_HYENA]
    h_bwd = h[:, :, D_HYENA:]
    g_lin = jnp.concatenate([jnp.zeros((depth, 1, D_HYENA), F32), h_bwd[:, :0:-1], h_fwd], axis=1)
    return g_lin.transpose(0, 2, 1).reshape(depth, D_HYENA, 2 * length // LANE, LANE)


def _lconv_kernel(nb, z_ref, g_ref, y_ref, zl_ref):
    krow = lax.broadcasted_iota(I32, (LANE, LANE), 0)
    acol = lax.broadcasted_iota(I32, (LANE, LANE), 1)
    upper = acol >= krow

    def shifted(c, seg):
        return pltpu.roll(jnp.broadcast_to(g_ref[c, seg:seg + 1, :], (LANE, LANE)), 0, 1, stride=1, stride_axis=0)

    def per_channel(c, carry):
        for j in range(nb):
            zl_ref[j * BATCH:(j + 1) * BATCH, :] = z_ref[c, :, j * LANE:(j + 1) * LANE]
        y_ref[c] = jnp.zeros((nb * BATCH, LANE), F32)
        prev = shifted(c, 0)
        for d in range(1 - nb, nb):
            cur = shifted(c, d + nb)
            toep = jnp.where(upper, cur, prev).astype(BF16)
            j0, j1 = max(0, -d), min(nb, nb - d)
            part = jnp.dot(zl_ref[j0 * BATCH:j1 * BATCH, :].astype(BF16), toep, preferred_element_type=F32)
            y_ref[c, (j0 + d) * BATCH:(j1 + d) * BATCH, :] += part
            prev = cur
        return carry

    lax.fori_loop(0, LC_CH, per_channel, 0)


def _long_conv(zt, g):
    ch, _, length = zt.shape
    nb = length // LANE
    return pl.pallas_call(
        functools.partial(_lconv_kernel, nb),
        grid=(ch // LC_CH,),
        in_specs=[
            pl.BlockSpec((LC_CH, BATCH, length), lambda i: (i, 0, 0)),
            pl.BlockSpec((LC_CH, 2 * nb, LANE), lambda i: (i, 0, 0)),
        ],
        out_specs=pl.BlockSpec((LC_CH, nb * BATCH, LANE), lambda i: (i, 0, 0)),
        out_shape=jax.ShapeDtypeStruct((ch, nb * BATCH, LANE), F32),
        scratch_shapes=[pltpu.VMEM((nb * BATCH, LANE), F32)],
        compiler_params=_cparams(("arbitrary",)),
        name="hyena_long_conv",
    )(zt, g)


def _hyena_conv(z, g, seq, row_off):
    zt = z[row_off:row_off + BATCH * seq].reshape(BATCH, seq, D_HYENA).transpose(2, 0, 1)
    yt = _long_conv(zt, g)
    nb = seq // LANE
    return yt.reshape(D_HYENA, nb, BATCH, LANE).transpose(2, 1, 3, 0).reshape(BATCH * seq, D_HYENA)


def _na_bias_index():
    rows = SEQ // GRID_W
    ngrp = rows // NA_QROWS
    qcol = np.arange(GRID_W)
    cstart = np.clip(qcol - NA_WIN_COLS // 2, 0, GRID_W - NA_WIN_COLS)
    ridx = np.zeros((3, NA_QROWS, NA_KROWS), np.int32)
    valid = np.zeros((3, NA_QROWS, GRID_W, NA_KROWS, GRID_W), bool)
    for v, g in enumerate((0, 1, ngrp - 1)):
        u0 = int(np.clip(NA_QROWS * g - NA_WIN_ROWS // 2, 0, rows - NA_KROWS))
        for ri in range(NA_QROWS):
            r = NA_QROWS * g + ri
            rs = int(np.clip(r - NA_WIN_ROWS // 2, 0, rows - NA_WIN_ROWS))
            kr = u0 + np.arange(NA_KROWS)
            row_ok = (kr >= rs) & (kr < rs + NA_WIN_ROWS)
            col_ok = (qcol[None, :] >= cstart[:, None]) & (qcol[None, :] < cstart[:, None] + NA_WIN_COLS)
            ok = row_ok[None, :, None] & col_ok[:, None, :]
            valid[v, ri] = ok
            ridx[v, ri] = np.clip(kr - r + NA_WIN_ROWS - 1, 0, 2 * NA_WIN_ROWS - 2)
    rel = np.clip(qcol[None, :] - qcol[:, None] + NA_WIN_COLS - 1, 0, 2 * NA_WIN_COLS - 2)
    onehot = (rel.reshape(1, -1) == np.arange(2 * NA_WIN_COLS - 1)[:, None]).astype(np.float32)
    nq, nk = NA_QROWS * GRID_W, NA_KROWS * GRID_W
    return ridx, onehot, valid.reshape(3, nq, nk)


def _na_bias(rpb):
    ridx, onehot, valid = _na_bias_index()
    nrow = 2 * NA_WIN_ROWS - 1
    cols = jnp.dot(rpb.reshape(NA_HEADS * nrow, -1), jnp.asarray(onehot), precision=HIGHEST)
    cols = cols.reshape(NA_HEADS, nrow, GRID_W, GRID_W)
    b = cols[:, ridx]
    b = b.transpose(1, 0, 2, 4, 3, 5).reshape(3, NA_HEADS, NA_QROWS * GRID_W, NA_KROWS * GRID_W)
    return jnp.where(valid[:, None], b, NEG_INF)


def _softmax_av(s_list, v_list):
    m = functools.reduce(jnp.maximum, [s.max(axis=-1, keepdims=True) for s in s_list])
    ps = [jnp.exp(s - m) for s in s_list]
    den = functools.reduce(jnp.add, [p.sum(axis=-1, keepdims=True) for p in ps])
    o = functools.reduce(jnp.add, [jnp.dot(p.astype(BF16), v, preferred_element_type=F32) for p, v in zip(ps, v_list)])
    return o / den


def _qk(q, k):
    return lax.dot_general(q, k, (((1,), (1,)), ((), ())), preferred_element_type=F32)


def _na_kernel(q_ref, k_ref, v_ref, kc_ref, vc_ref, bias_ref, prev_ref, o_ref):
    del prev_ref
    rows = SEQ // GRID_W
    g = pl.program_id(1)
    u0 = jnp.clip(NA_QROWS * g - NA_WIN_ROWS // 2, 0, rows - NA_KROWS)
    start = pl.multiple_of(u0 * GRID_W, GRID_W)
    nk = NA_KROWS * GRID_W
    scale = NA_HEAD_DIM ** -0.5
    for h in range(NA_HEADS):
        sl = slice(h * NA_HEAD_DIM, (h + 1) * NA_HEAD_DIM)
        qh = q_ref[:, sl]
        kh = k_ref[pl.ds(start, nk), sl]
        vh = v_ref[pl.ds(start, nk), sl]
        s_loc = _qk(qh, kh) * scale + bias_ref[0, h]
        s_ctx = _qk(qh, kc_ref[:, sl]) * scale
        o_ref[:, sl] = _softmax_av([s_loc, s_ctx], [vh, vc_ref[:, sl]]).astype(o_ref.dtype)


def _na_attention(q, k, v, bias):
    nq = NA_QROWS * GRID_W
    ngrp = SEQ // nq
    ctx0 = N_LAT // CTX_LEN

    def variant(b, g):
        return (jnp.where(g == 0, 0, jnp.where(g == ngrp - 1, 2, 1)), 0, 0, 0)

    return pl.pallas_call(
        _na_kernel,
        grid=(BATCH, ngrp),
        in_specs=[
            pl.BlockSpec((nq, D_NA), lambda b, g: (b * ngrp + g, 0)),
            pl.BlockSpec((SEQ, D_NA), lambda b, g: (b, 0)),
            pl.BlockSpec((SEQ, D_NA), lambda b, g: (b, 0)),
            pl.BlockSpec((CTX_LEN, D_NA), lambda b, g: (ctx0 + b, 0)),
            pl.BlockSpec((CTX_LEN, D_NA), lambda b, g: (ctx0 + b, 0)),
            pl.BlockSpec((1, NA_HEADS, nq, NA_KROWS * GRID_W), variant),
            pl.BlockSpec(memory_space=pl.ANY),
        ],
        out_specs=pl.BlockSpec((nq, D_NA), lambda b, g: (b * ngrp + g, 0)),
        out_shape=jax.ShapeDtypeStruct((N_TOK, D_NA), BF16),
        input_output_aliases={6: 0},
        compiler_params=_cparams(("arbitrary", "arbitrary")),
        name="na_attention",
    )(q, k, v, k, v, bias, jnp.zeros((N_TOK, D_NA), BF16))


def _ctx_attn_kernel(q_ref, k_ref, v_ref, prev_ref, o_ref):
    del prev_ref
    scale = NA_HEAD_DIM ** -0.5
    for h in range(NA_HEADS):
        sl = slice(h * NA_HEAD_DIM, (h + 1) * NA_HEAD_DIM)
        s = _qk(q_ref[:, sl], k_ref[:, sl]) * scale
        o_ref[:, sl] = _softmax_av([s], [v_ref[:, sl]]).astype(o_ref.dtype)


def _ctx_attention(q, k, v, att):
    ctx0 = N_LAT // CTX_LEN
    sp = pl.BlockSpec((CTX_LEN, D_NA), lambda b: (ctx0 + b, 0))
    return pl.pallas_call(
        _ctx_attn_kernel,
        grid=(BATCH,),
        in_specs=[sp, sp, sp, pl.BlockSpec(memory_space=pl.ANY)],
        out_specs=sp,
        out_shape=jax.ShapeDtypeStruct(att.shape, att.dtype),
        input_output_aliases={3: 0},
        compiler_params=_cparams(("arbitrary",)),
        name="ctx_attention",
    )(q, k, v, att)


def _mix_kernel(yc_ref, z_ref, x0_ref, ysc_ref, ya_ref, gt_ref, xl_ref, g1_ref, sh2_ref, sc2_ref, bd_ref,
                wh_ref, ws_ref, wn_ref, wo_ref, lg_ref, lb_ref, xo_ref, ho_ref):
    dot = functools.partial(jnp.dot, preferred_element_type=F32)
    y_hy = x0_ref[...] * (yc_ref[...] + z_ref[...] * bd_ref[...])
    merged = (jax.nn.sigmoid(gt_ref[:, 0:D_MODEL]) * dot(y_hy.astype(BF16), wh_ref[...])
              + jax.nn.sigmoid(gt_ref[:, D_MODEL:2 * D_MODEL]) * dot(ysc_ref[...].astype(BF16), ws_ref[...])
              + jax.nn.sigmoid(gt_ref[:, 2 * D_MODEL:3 * D_MODEL]) * dot(ya_ref[...], wn_ref[...]))
    out = dot(merged.astype(BF16), wo_ref[...])
    xo = _layer_norm(ALPHA * xl_ref[...] + g1_ref[0] * out, lg_ref[...], lb_ref[...])
    xo_ref[...] = xo
    ho_ref[...] = xo * (1.0 + sc2_ref[0]) + sh2_ref[0]


def _mix(yc, z, x0, ysc, ya, gates, xl, g1, sh2, sc2, bias_d, wh, ws, wn, wo, ln_g, ln_b, n_tok):
    tm = TOK_TILE
    tok = lambda n: pl.BlockSpec((tm, n), lambda t: (t, 0))
    mod = pl.BlockSpec((1, 1, D_MODEL), lambda t: (_mod_row(t, SEQ // tm, N_LAT // tm), 0, 0))
    const = lambda r, n: pl.BlockSpec((r, n), lambda t: (0, 0))
    return pl.pallas_call(
        _mix_kernel,
        grid=(n_tok // tm,),
        in_specs=[tok(D_HYENA), tok(D_HYENA), tok(D_HYENA), tok(D_SCONV), tok(D_NA), tok(3 * D_MODEL), tok(D_MODEL),
                  mod, mod, mod, const(1, D_HYENA),
                  const(D_HYENA, D_MODEL), const(D_SCONV, D_MODEL), const(D_NA, D_MODEL), const(D_MODEL, D_MODEL),
                  const(1, D_MODEL), const(1, D_MODEL)],
        out_specs=[tok(D_MODEL), tok(D_MODEL)],
        out_shape=[jax.ShapeDtypeStruct((xl.shape[0], D_MODEL), F32)] * 2,
        compiler_params=_cparams(("arbitrary",)),
        name="mixer_out",
    )(yc, z, x0, ysc, ya, gates, xl, g1, sh2, sc2, bias_d, wh, ws, wn, wo, ln_g, ln_b)


def _router_kernel(h_ref, wr_ref, rb_ref, tri_ref, te_ref, rk_ref, w_ref, cnt_ref, run_ref):
    tm = h_ref.shape[0]
    per = N_EXPERTS // N_GROUPS

    @pl.when(pl.program_id(0) == 0)
    def _():
        run_ref[...] = jnp.zeros_like(run_ref)

    logits = lax.dot_general(wr_ref[...], h_ref[...], (((1,), (1,)), ((), ())),
                             precision=HIGHEST, preferred_element_type=F32)
    scores = jax.nn.sigmoid(logits)
    sel = scores + rb_ref[...]
    sub = lax.broadcasted_iota(I32, (per, tm), 0)
    colmax = lambda a: jnp.max(a, axis=0, keepdims=True)
    colmin = lambda a: jnp.min(a, axis=0, keepdims=True)
    ninf = -jnp.inf

    xs = [sel[g * per:(g + 1) * per, :] for g in range(N_GROUPS)]
    sc = [scores[g * per:(g + 1) * per, :] for g in range(N_GROUPS)]
    gs = []
    for x in xs:
        m1 = colmax(x)
        i1 = colmin(jnp.where(x == m1, sub, per))
        m2 = colmax(jnp.where(sub == i1, ninf, x))
        gs.append(m1 + m2)
    chosen = [jnp.zeros((1, tm), F32) for _ in range(N_GROUPS)]
    for _ in range(TOPK_GROUPS):
        gm = functools.reduce(jnp.maximum, gs)
        gi = jnp.full((1, tm), N_GROUPS, I32)
        for g in reversed(range(N_GROUPS)):
            gi = jnp.where(gs[g] == gm, g, gi)
        for g in range(N_GROUPS):
            hit = gi == g
            chosen[g] = jnp.where(hit, 1.0, chosen[g])
            gs[g] = jnp.where(hit, ninf, gs[g])
    xm = [jnp.where(jnp.broadcast_to(chosen[g], (per, tm)) > 0.5, xs[g], ninf) for g in range(N_GROUPS)]
    eidx = [sub + g * per for g in range(N_GROUPS)]
    picked = [jnp.zeros((per, tm), F32) for _ in range(N_GROUPS)]
    top_e, top_s = [], []
    for _ in range(TOP_K):
        em = functools.reduce(jnp.maximum, [colmax(x) for x in xm])
        ei = functools.reduce(jnp.minimum, [colmin(jnp.where(xm[g] == em, eidx[g], N_EXPERTS)) for g in range(N_GROUPS)])
        s_acc = jnp.zeros((1, tm), F32)
        for g in range(N_GROUPS):
            hit = eidx[g] == ei
            picked[g] = jnp.where(hit, 1.0, picked[g])
            xm[g] = jnp.where(hit, ninf, xm[g])
            s_acc = s_acc + jnp.sum(jnp.where(hit, sc[g], 0.0), axis=0, keepdims=True)
        top_e.append(ei)
        top_s.append(s_acc)
    den = functools.reduce(jnp.add, top_s)
    mask = jnp.concatenate(picked, axis=0)
    prefix = jnp.dot(mask.astype(BF16), tri_ref[...], preferred_element_type=F32)
    pos = run_ref[:, 0:1] + prefix
    posg = [pos[g * per:(g + 1) * per, :] for g in range(N_GROUPS)]
    for i in range(TOP_K):
        r_acc = jnp.zeros((1, tm), F32)
        for g in range(N_GROUPS):
            r_acc = r_acc + jnp.sum(jnp.where(eidx[g] == top_e[i], posg[g], 0.0), axis=0, keepdims=True)
        te_ref[0, i:i + 1, :] = top_e[i]
        rk_ref[0, i:i + 1, :] = r_acc.astype(I32)
        w_ref[0, i:i + 1, :] = top_s[i] / den * ROUTED_SCALE
    run_ref[...] = run_ref[...] + jnp.sum(mask, axis=1, keepdims=True)
    cnt_ref[...] = run_ref[...]


def _route(tokens, router_w_t, router_b, n_tok):
    tm = RT_TILE
    nt = n_tok // tm
    tri = jnp.asarray(np.triu(np.ones((tm, tm), np.float32), 1), BF16)
    out3 = lambda dt: jax.ShapeDtypeStruct((nt, TOP_K, tm), dt)
    osp = pl.BlockSpec((1, TOP_K, tm), lambda t: (t, 0, 0))
    return pl.pallas_call(
        _router_kernel,
        grid=(nt,),
        in_specs=[
            pl.BlockSpec((tm, D_MODEL), lambda t: (t, 0)),
            pl.BlockSpec((N_EXPERTS, D_MODEL), lambda t: (0, 0)),
            pl.BlockSpec((N_EXPERTS, 1), lambda t: (0, 0)),
            pl.BlockSpec((tm, tm), lambda t: (0, 0)),
        ],
        out_specs=[osp, osp, osp, pl.BlockSpec((N_EXPERTS, LANE), lambda t: (0, 0))],
        out_shape=[out3(I32), out3(I32), out3(F32), jax.ShapeDtypeStruct((N_EXPERTS, LANE), F32)],
        scratch_shapes=[pltpu.VMEM((N_EXPERTS, LANE), F32)],
        compiler_params=_cparams(("arbitrary",)),
        name="moe_router",
    )(tokens, router_w_t, router_b, tri)


def _dispatch_kernel(cnt_ref, pst_ref, na_ref, dest_ref, h_ref, xs_hbm, zero_ref, sem):
    tm = dest_ref.shape[2]
    row_copy = lambda src, dst: pltpu.make_async_copy(src, xs_hbm.at[pl.ds(dst, 1)], sem)
    n_blocks = xs_hbm.shape[0] // MOE_BLOCK

    @pl.when(pl.program_id(0) == 0)
    def _():
        zero_ref[...] = jnp.zeros_like(zero_ref)

        def blk_copy(blk):
            return pltpu.make_async_copy(zero_ref, xs_hbm.at[pl.ds(pl.multiple_of(blk * MOE_BLOCK, MOE_BLOCK), MOE_BLOCK)], sem)

        def fill_blk(blk, c):
            blk_copy(blk).start()
            return c

        def drain_blk(blk, c):
            blk_copy(blk).wait()
            return c

        lax.fori_loop(na_ref[0], n_blocks, fill_blk, 0)
        lax.fori_loop(na_ref[0], n_blocks, drain_blk, 0)

        def per_expert(e, carry):
            n = cnt_ref[e]
            n_pad = (n + MOE_BLOCK - 1) // MOE_BLOCK * MOE_BLOCK - n
            base = pst_ref[e] + n

            def fill(r, c):
                row_copy(zero_ref.at[pl.ds(0, 1)], base + r).start()
                return c

            def drain(r, c):
                row_copy(zero_ref.at[pl.ds(0, 1)], base).wait()
                return c

            lax.fori_loop(0, n_pad, fill, 0)
            lax.fori_loop(0, n_pad, drain, 0)
            return carry

        lax.fori_loop(0, N_EXPERTS, per_expert, 0)

    def issue(t, carry):
        for i in range(TOP_K):
            row_copy(h_ref.at[pl.ds(t, 1)], dest_ref[0, i, t]).start()
        return carry

    def drain(t, carry):
        for i in range(TOP_K):
            row_copy(h_ref.at[pl.ds(0, 1)], 0).wait()
        return carry

    lax.fori_loop(0, tm, issue, 0, unroll=2)
    lax.fori_loop(0, tm, drain, 0, unroll=2)


def _dispatch(counts, pstart, n_active, dest, tokens, n_slots):
    nt, _, tm = dest.shape
    grid_spec = pltpu.PrefetchScalarGridSpec(
        num_scalar_prefetch=3,
        grid=(nt,),
        in_specs=[
            pl.BlockSpec((1, TOP_K, tm), lambda t, c, p, a: (t, 0, 0), memory_space=pltpu.SMEM),
            pl.BlockSpec((tm, D_MODEL), lambda t, c, p, a: (t, 0)),
        ],
        out_specs=pl.BlockSpec(memory_space=pl.ANY),
        scratch_shapes=[pltpu.VMEM((MOE_BLOCK, D_MODEL), F32), pltpu.SemaphoreType.DMA],
    )
    return pl.pallas_call(
        _dispatch_kernel,
        grid_spec=grid_spec,
        out_shape=jax.ShapeDtypeStruct((n_slots, D_MODEL), F32),
        compiler_params=_cparams(("arbitrary",)),
        name="moe_dispatch",
    )(counts, pstart, n_active, dest, tokens)


def _expert_kernel(be_ref, na_ref, x_ref, w1_ref, w3_ref, w2_ref, y_ref):
    active = pl.program_id(0) < na_ref[0]

    @pl.when(active)
    def _():
        dot = functools.partial(jnp.dot, preferred_element_type=F32)
        x = x_ref[...].astype(BF16)
        a = dot(x, w1_ref[0])
        hidden = (a * jax.nn.sigmoid(a)) * dot(x, w3_ref[0])
        y_ref[...] = dot(hidden.astype(BF16), w2_ref[0])

    @pl.when(jnp.logical_not(active))
    def _():
        y_ref[...] = jnp.zeros_like(y_ref)


def _experts(blk_expert, n_active, xs, w1, w3, w2):
    n_blocks = xs.shape[0] // MOE_BLOCK
    xmap = lambda i, be, na: (jnp.minimum(i, na[0] - 1), 0)
    wmap = lambda i, be, na: (be[i], 0, 0)
    grid_spec = pltpu.PrefetchScalarGridSpec(
        num_scalar_prefetch=2,
        grid=(n_blocks,),
        in_specs=[
            pl.BlockSpec((MOE_BLOCK, D_MODEL), xmap),
            pl.BlockSpec((1, D_MODEL, D_EXPERT), wmap),
            pl.BlockSpec((1, D_MODEL, D_EXPERT), wmap),
            pl.BlockSpec((1, D_EXPERT, D_MODEL), wmap),
        ],
        out_specs=pl.BlockSpec((MOE_BLOCK, D_MODEL), lambda i, be, na: (i, 0)),
    )
    return pl.pallas_call(
        _expert_kernel,
        grid_spec=grid_spec,
        out_shape=jax.ShapeDtypeStruct(xs.shape, F32),
        compiler_params=_cparams(("arbitrary",)),
        name="moe_experts",
    )(blk_expert, n_active, xs, w1, w3, w2)


def _combine_kernel(dest_ref, w_ref, h_ref, xl_ref, g2_ref, s1_ref, s3_ref, s2_ref, lg_ref, lb_ref, ys_hbm,
                    xo_ref, ybuf, sem):
    tm = h_ref.shape[0]

    def row_copy(i, t, src):
        return pltpu.make_async_copy(ys_hbm.at[pl.ds(src, 1)], ybuf.at[i, pl.ds(t, 1)], sem)

    def issue(t, carry):
        for i in range(TOP_K):
            row_copy(i, t, dest_ref[0, i, t]).start()
        return carry

    def drain(t, carry):
        for i in range(TOP_K):
            row_copy(i, t, 0).wait()
        return carry

    lax.fori_loop(0, tm, issue, 0)
    dot = functools.partial(jnp.dot, preferred_element_type=F32)
    hb = h_ref[...].astype(BF16)
    a = dot(hb, s1_ref[...])
    f = dot(((a * jax.nn.sigmoid(a)) * dot(hb, s3_ref[...])).astype(BF16), s2_ref[...])
    lax.fori_loop(0, tm, drain, 0)
    for i in range(TOP_K):
        f = f + w_ref[:, i:i + 1] * ybuf[i]
    xo_ref[...] = _layer_norm(ALPHA * xl_ref[...] + g2_ref[0] * f, lg_ref[...], lb_ref[...])


def _combine(dest, w, h, xl, g2, s1, s3, s2, ln_g, ln_b, ys, n_tok):
    tm = CB_TILE
    nt = n_tok // tm
    tok = lambda n: pl.BlockSpec((tm, n), lambda t: (t, 0))
    const = lambda r, n: pl.BlockSpec((r, n), lambda t: (0, 0))
    return pl.pallas_call(
        _combine_kernel,
        grid=(nt,),
        in_specs=[
            pl.BlockSpec((1, TOP_K, tm), lambda t: (t, 0, 0), memory_space=pltpu.SMEM),
            tok(TOP_K), tok(D_MODEL), tok(D_MODEL),
            pl.BlockSpec((1, 1, D_MODEL), lambda t: (_mod_row(t, SEQ // tm, N_LAT // tm), 0, 0)),
            const(D_MODEL, D_SHARED), const(D_MODEL, D_SHARED), const(D_SHARED, D_MODEL),
            const(1, D_MODEL), const(1, D_MODEL),
            pl.BlockSpec(memory_space=pl.ANY),
        ],
        out_specs=tok(D_MODEL),
        out_shape=jax.ShapeDtypeStruct((n_tok, D_MODEL), F32),
        scratch_shapes=[pltpu.VMEM((TOP_K, tm, D_MODEL), F32), pltpu.SemaphoreType.DMA],
        compiler_params=_cparams(("arbitrary",)),
        name="moe_combine",
    )(dest, w, h, xl, g2, s1, s3, s2, ln_g, ln_b, ys)


def _moe_layer(h, xl, g2, router_w_t, router_b, w1, w3, w2, s1, s3, s2, ln_g, ln_b, n_tok):
    top_e, rank, wts, cnt = _route(h, router_w_t, router_b, n_tok)
    counts = cnt[:, 0].astype(I32)
    padded = (counts + MOE_BLOCK - 1) // MOE_BLOCK * MOE_BLOCK
    pend = jnp.cumsum(padded)
    pstart = pend - padded
    n_blocks = n_tok * TOP_K // MOE_BLOCK + N_EXPERTS
    onehot = top_e[..., None] == jnp.arange(N_EXPERTS, dtype=I32)
    dest = rank + jnp.sum(jnp.where(onehot, pstart, 0), axis=-1)
    blk_expert = jnp.minimum(
        jnp.sum((pend // MOE_BLOCK)[None, :] <= jnp.arange(n_blocks, dtype=I32)[:, None], axis=1), N_EXPERTS - 1).astype(I32)
    n_active = (pend[-1:] // MOE_BLOCK).astype(I32)
    xs = _dispatch(counts, pstart.astype(I32), n_active, dest, h, n_blocks * MOE_BLOCK)
    ys = _experts(blk_expert, n_active, xs, w1, w3, w2)
    nt = n_tok // CB_TILE
    regroup = lambda a: a.transpose(1, 0, 2).reshape(TOP_K, n_tok)
    dest_c = regroup(dest).reshape(TOP_K, nt, CB_TILE).transpose(1, 0, 2)
    w_c = regroup(wts).T
    return _combine(dest_c, w_c, h, xl, g2, s1, s3, s2, ln_g, ln_b, ys, n_tok)


def kernel(x, c, ctx, c_ctx, w_ada, b_ada, w_in, hy_conv_w, hy_conv_b, hy_w1, hy_b1, hy_w2, hy_b2, hy_w3, hy_sin_freq, hy_bias_d, hy_proj, sc_conv_w, sc_proj, na_rpb, na_proj, w_o, ln1_g, ln1_b, ln2_g, ln2_b, moe_router, moe_bias, moe_w1, moe_w3, moe_w2, sh_w1, sh_w3, sh_w2):
    bf = lambda a: a.astype(BF16)
    xa = jnp.concatenate([x.reshape(N_LAT, D_MODEL), ctx.reshape(N_CTX, D_MODEL)], axis=0)

    cc = jnp.zeros((16, D_MODEL), F32).at[:BATCH].set(c).at[BATCH].set(c_ctx)
    mods = _ada(cc, w_ada, b_ada)[:, :BATCH + 1].reshape(DEPTH, BATCH + 1, 6, 1, D_MODEL)
    filt_lat = _hyena_filters(SEQ, hy_w1, hy_b1, hy_w2, hy_b2, hy_w3, hy_sin_freq)
    filt_ctx = _hyena_filters(CTX_LEN, hy_w1, hy_b1, hy_w2, hy_b2, hy_w3, hy_sin_freq)

    offs = [0, 3 * D_HYENA, 3 * D_HYENA + 3 * D_SCONV]
    offs += [offs[2] + D_NA, offs[2] + 2 * D_NA, offs[2] + 3 * D_NA, w_in.shape[2]]

    for i in range(DEPTH):
        last = i == DEPTH - 1
        m = [mods[i, :, j] for j in range(6)]
        w_secs = [bf(w_in[i][:, offs[j]:offs[j + 1]]) for j in range(6)]
        u_hy, u_sc, q, k, v, gates = _proj(xa, m[0], m[1], w_secs, N_TOK)

        z, x0 = _hy_pre(u_hy, hy_conv_w[i], hy_conv_b[i][None], None, SEQ, 0)
        z, x0 = _hy_pre(u_hy, hy_conv_w[i], hy_conv_b[i][None], (z, x0), CTX_LEN, N_LAT)
        yc = jnp.concatenate([_hyena_conv(z, filt_lat[i], SEQ, 0), _hyena_conv(z, filt_ctx[i], CTX_LEN, N_LAT)], axis=0)
        ysc = _short_conv(u_sc, sc_conv_w[i], None, SEQ, 0)
        ysc = _short_conv(u_sc, sc_conv_w[i], ysc, CTX_LEN, N_LAT)
        att = _na_attention(q, k, v, _na_bias(na_rpb[i]))
        att = _ctx_attention(q, k, v, att)

        xa, hmoe = _mix(yc, z, x0, ysc, att, gates, xa, m[2], m[3], m[4], hy_bias_d[i][None],
                        bf(hy_proj[i]), bf(sc_proj[i]), bf(na_proj[i]), bf(w_o[i]), ln1_g[i][None], ln1_b[i][None], N_TOK)

        n_moe = N_LAT if last else N_TOK
        xa = _moe_layer(hmoe, xa, m[5], moe_router[i].T, moe_bias[i][:, None], bf(moe_w1[i]), bf(moe_w3[i]),
                        bf(moe_w2[i]), bf(sh_w1[i]), bf(sh_w3[i]), bf(sh_w2[i]), ln2_g[i][None], ln2_b[i][None], n_moe)
    return xa.reshape(BATCH, SEQ, D_MODEL)
```

```python
import functools
import math

import numpy as np
import jax
import jax.numpy as jnp
from jax import lax
from jax.experimental import pallas as pl
from jax.experimental.pallas import tpu as pltpu

F32 = jnp.float32
BF16 = jnp.bfloat16
I32 = jnp.int32
HIGHEST = lax.Precision.HIGHEST

D_MODEL = 1024
BATCH = 8
SEQ = 4096
DEPTH = 4
CTX_LEN = 256
GRID_W = 64
D_HYENA = 512
D_SCONV = 512
NA_HEADS = 8
NA_HEAD_DIM = 64
D_NA = NA_HEADS * NA_HEAD_DIM
NA_WIN_ROWS = 8
NA_WIN_COLS = 16
HY_BANDS = 16
HY_EMB = 1 + 2 * HY_BANDS
HY_FILTER_DIM = 64
HY_FAST_DECAY = 0.3
HY_SLOW_DECAY = 1.5
HY_TARGET = 1e-2
N_EXPERTS = 64
N_GROUPS = 8
TOPK_GROUPS = 4
TOP_K = 8
D_EXPERT = 256
D_SHARED = 256
ROUTED_SCALE = 2.5
LN_EPS = 1e-5
NEG_INF = -1e30
ALPHA = (2.0 * DEPTH) ** 0.25

N_LAT = BATCH * SEQ
N_CTX = BATCH * CTX_LEN
N_TOK = N_LAT + N_CTX

LANE = 128
SUBLANE = 8
VMEM_LIMIT = 56 * 1024 * 1024

TOK_TILE = 256
NA_QROWS = 4
NA_KROWS = NA_QROWS + NA_WIN_ROWS - 1
MOE_BLOCK = 256
RT_TILE = 512
CB_TILE = 128
LC_CH = 8
ROW_UNROLL = 4
WAIT_UNROLL = 16


def _cparams(sem):
    return pltpu.CompilerParams(dimension_semantics=sem, vmem_limit_bytes=VMEM_LIMIT)


def _mod_row(tile, tiles_per_batch, n_lat_tiles):
    return jnp.where(tile < n_lat_tiles, tile // tiles_per_batch, BATCH)


def _layer_norm(v, g, b):
    mu = jnp.mean(v, axis=-1, keepdims=True)
    c = v - mu
    var = jnp.mean(c * c, axis=-1, keepdims=True)
    return c * lax.rsqrt(var + LN_EPS) * g + b


def _ada_kernel(c_ref, w_ref, b_ref, o_ref):
    c = c_ref[...]
    cond = c * jax.nn.sigmoid(c)
    o_ref[0] = jnp.dot(cond, w_ref[0], precision=HIGHEST, preferred_element_type=F32) + b_ref[0]


def _ada(cc, w_ada, b_ada):
    depth, d, n = w_ada.shape
    tn = 1536
    return pl.pallas_call(
        _ada_kernel,
        grid=(depth, n // tn),
        in_specs=[
            pl.BlockSpec((16, d), lambda l, j: (0, 0)),
            pl.BlockSpec((1, d, tn), lambda l, j: (l, 0, j)),
            pl.BlockSpec((1, 1, tn), lambda l, j: (l, 0, j)),
        ],
        out_specs=pl.BlockSpec((1, 16, tn), lambda l, j: (l, 0, j)),
        out_shape=jax.ShapeDtypeStruct((depth, 16, n), F32),
        compiler_params=_cparams(("arbitrary", "arbitrary")),
        name="ada",
    )(cc, w_ada, b_ada.reshape(depth, 1, n))


def _proj_kernel(x_ref, sh_ref, sc_ref, w_hy, w_sc, w_q, w_k, w_v, w_g, o_hy, o_sc, o_q, o_k, o_v, o_g):
    h = (x_ref[...] * (1.0 + sc_ref[0]) + sh_ref[0]).astype(BF16)
    for w, o in ((w_hy, o_hy), (w_sc, o_sc), (w_q, o_q), (w_k, o_k), (w_v, o_v), (w_g, o_g)):
        o[...] = jnp.dot(h, w[...], preferred_element_type=F32).astype(o.dtype)


def _proj(x, shift, scale, ws, n_tok):
    tm = TOK_TILE
    nt = n_tok // tm
    mod = lambda t: (_mod_row(t, SEQ // tm, N_LAT // tm), 0, 0)
    widths = [w.shape[1] for w in ws]
    dtypes = [F32, F32, BF16, BF16, BF16, F32]
    return pl.pallas_call(
        _proj_kernel,
        grid=(nt,),
        in_specs=[
            pl.BlockSpec((tm, D_MODEL), lambda t: (t, 0)),
            pl.BlockSpec((1, 1, D_MODEL), mod),
            pl.BlockSpec((1, 1, D_MODEL), mod),
        ] + [pl.BlockSpec((D_MODEL, n), lambda t: (0, 0), pipeline_mode=pl.Buffered(1)) for n in widths],
        out_specs=[pl.BlockSpec((tm, n), lambda t: (t, 0)) for n in widths],
        out_shape=[jax.ShapeDtypeStruct((x.shape[0], n), dt) for n, dt in zip(widths, dtypes)],
        compiler_params=_cparams(("arbitrary",)),
        name="in_proj",
    )(x, shift, scale, *ws)


def _dwconv3(u, w):
    s = u.shape[0]
    row = lax.broadcasted_iota(I32, u.shape, 0)
    prev = jnp.where(row == 0, 0.0, pltpu.roll(u, 1, 0))
    nxt = jnp.where(row == s - 1, 0.0, pltpu.roll(u, s - 1, 0))
    return prev * w[0:1] + u * w[1:2] + nxt * w[2:3]


def _hy_pre_kernel(u0_ref, u1_ref, u2_ref, w0_ref, w1_ref, w2_ref, b0_ref, b1_ref, b2_ref, pz_ref, px_ref,
                   z_ref, x0_ref):
    del pz_ref, px_ref
    x0_ref[...] = _dwconv3(u0_ref[...], w0_ref[...]) + b0_ref[...]
    x1 = _dwconv3(u1_ref[...], w1_ref[...]) + b1_ref[...]
    v = _dwconv3(u2_ref[...], w2_ref[...]) + b2_ref[...]
    z_ref[...] = v * x1


def _hy_pre(u, conv_w, conv_b, prev, seq, row_off):
    ncb = D_HYENA // LANE
    ob = row_off // seq
    usp = lambda s: pl.BlockSpec((seq, LANE), lambda b, c: (ob + b, s * ncb + c))
    wsp = lambda s: pl.BlockSpec((3, LANE), lambda b, c: (0, s * ncb + c))
    bsp = lambda s: pl.BlockSpec((1, LANE), lambda b, c: (0, s * ncb + c))
    osp = pl.BlockSpec((seq, LANE), lambda b, c: (ob + b, c))
    n_tok = u.shape[0]
    if prev is None:
        prev = (jnp.zeros((n_tok, D_HYENA), F32), jnp.zeros((n_tok, D_HYENA), F32))
    args = [u, u, u, conv_w, conv_w, conv_w, conv_b, conv_b, conv_b] + list(prev)
    in_specs = [usp(0), usp(1), usp(2), wsp(0), wsp(1), wsp(2), bsp(0), bsp(1), bsp(2)]
    in_specs += [pl.BlockSpec(memory_space=pl.ANY)] * 2
    return pl.pallas_call(
        _hy_pre_kernel,
        grid=(BATCH, ncb),
        in_specs=in_specs,
        out_specs=[osp, osp],
        out_shape=[jax.ShapeDtypeStruct((n_tok, D_HYENA), F32)] * 2,
        input_output_aliases={9: 0, 10: 1},
        compiler_params=_cparams(("arbitrary", "arbitrary")),
        name="hyena_pre",
    )(*args)


def _sc_kernel(bg_ref, cg_ref, xs_ref, w_ref, prev_ref, o_ref):
    del prev_ref
    o_ref[...] = bg_ref[...] * _dwconv3(cg_ref[...] * xs_ref[...], w_ref[...])


def _short_conv(u, conv_w, prev, seq, row_off):
    ncb = D_SCONV // LANE
    ob = row_off // seq
    usp = lambda s: pl.BlockSpec((seq, LANE), lambda b, c: (ob + b, s * ncb + c))
    osp = pl.BlockSpec((seq, LANE), lambda b, c: (ob + b, c))
    if prev is None:
        prev = jnp.zeros((u.shape[0], D_SCONV), F32)
    in_specs = [usp(0), usp(1), usp(2), pl.BlockSpec((3, LANE), lambda b, c: (0, c)),
                pl.BlockSpec(memory_space=pl.ANY)]
    return pl.pallas_call(
        _sc_kernel,
        grid=(BATCH, ncb),
        in_specs=in_specs,
        out_specs=osp,
        out_shape=jax.ShapeDtypeStruct((u.shape[0], D_SCONV), F32),
        input_output_aliases={4: 0},
        compiler_params=_cparams(("arbitrary", "arbitrary")),
        name="short_conv",
    )(u, u, u, conv_w, prev)


def _filter_kernel(f_ref, dec_ref, w1_ref, b1_ref, w2_ref, b2_ref, w3_ref, fr_ref, o_ref):
    dot = functools.partial(jnp.dot, precision=HIGHEST, preferred_element_type=F32)
    z = jnp.sin(fr_ref[0, 0:1] * (dot(f_ref[...], w1_ref[0]) + b1_ref[0]))
    z = jnp.sin(fr_ref[0, 1:2] * (dot(z, w2_ref[0]) + b2_ref[0]))
    o_ref[0] = dot(z, w3_ref[0]) * dec_ref[...]


def _filter_consts(length):
    t = jnp.linspace(0.0, 1.0, length, dtype=F32)[:, None]
    ang = (2.0 * math.pi / length) * jnp.arange(length, dtype=F32)[:, None]
    bands = jnp.linspace(1e-4, HY_BANDS - 1, HY_BANDS, dtype=F32)[None, :]
    feats = jnp.concatenate([t, jnp.cos(bands * ang), -jnp.sin(bands * ang)], -1)
    feats = jnp.pad(feats, ((0, 0), (0, LANE - HY_EMB)))
    deltas = jnp.abs(jnp.linspace(math.log(HY_TARGET) / HY_SLOW_DECAY, math.log(HY_TARGET) / HY_FAST_DECAY,
                                  D_HYENA, dtype=F32))
    decay = jnp.exp(-t * deltas[None, :])
    return feats, jnp.concatenate([decay, decay], -1)


def _hyena_filters(length, w1, b1, w2, b2, w3, sin_freq):
    depth = w1.shape[0]
    pf = LANE - HY_FILTER_DIM
    w1p = jnp.pad(w1, ((0, 0), (0, LANE - HY_EMB), (0, pf)))
    b1p = jnp.pad(b1, ((0, 0), (0, pf)))[:, None, :]
    w2p = jnp.pad(w2, ((0, 0), (0, pf), (0, pf)))
    b2p = jnp.pad(b2, ((0, 0), (0, pf)))[:, None, :]
    w3p = jnp.pad(w3, ((0, 0), (0, pf), (0, 0)))
    frp = jnp.pad(sin_freq, ((0, 0), (0, 0), (0, pf)))
    feats, decay = _filter_consts(length)
    tl = min(length, 512)
    lsp = lambda shape: pl.BlockSpec((1,) + shape, lambda l, j: (l, 0, 0))
    h = pl.pallas_call(
        _filter_kernel,
        grid=(depth, length // tl),
        in_specs=[
            pl.BlockSpec((tl, LANE), lambda l, j: (j, 0)),
            pl.BlockSpec((tl, 2 * D_HYENA), lambda l, j: (j, 0)),
            lsp((LANE, LANE)), lsp((1, LANE)), lsp((LANE, LANE)), lsp((1, LANE)),
            lsp((LANE, 2 * D_HYENA)), lsp((2, LANE)),
        ],
        out_specs=pl.BlockSpec((1, tl, 2 * D_HYENA), lambda l, j: (l, j, 0)),
        out_shape=jax.ShapeDtypeStruct((depth, length, 2 * D_HYENA), F32),
        compiler_params=_cparams(("arbitrary", "arbitrary")),
        name="hyena_filter",
    )(feats, decay, w1p, b1p, w2p, b2p, w3p, frp)
    h_fwd = h[:, :, :D_HYENA]
    h_bwd = h[:, :, D_HYENA:]
    g_lin = jnp.concatenate([jnp.zeros((depth, 1, D_HYENA), F32), h_bwd[:, :0:-1], h_fwd], axis=1)
    return g_lin.transpose(0, 2, 1).reshape(depth, D_HYENA, 2 * length // LANE, LANE)


def _lconv_kernel(nb, z_ref, g_ref, y_ref, zl_ref):
    krow = lax.broadcasted_iota(I32, (LANE, LANE), 0)
    acol = lax.broadcasted_iota(I32, (LANE, LANE), 1)
    upper = acol >= krow

    def shifted(c, seg):
        return pltpu.roll(jnp.broadcast_to(g_ref[c, seg:seg + 1, :], (LANE, LANE)), 0, 1, stride=1, stride_axis=0)

    def per_channel(c, carry):
        for j in range(nb):
            zl_ref[j * BATCH:(j + 1) * BATCH, :] = z_ref[c, :, j * LANE:(j + 1) * LANE]
        y_ref[c] = jnp.zeros((nb * BATCH, LANE), F32)
        prev = shifted(c, 0)
        for d in range(1 - nb, nb):
            cur = shifted(c, d + nb)
            toep = jnp.where(upper, cur, prev).astype(BF16)
            j0, j1 = max(0, -d), min(nb, nb - d)
            part = jnp.dot(zl_ref[j0 * BATCH:j1 * BATCH, :].astype(BF16), toep, preferred_element_type=F32)
            y_ref[c, (j0 + d) * BATCH:(j1 + d) * BATCH, :] += part
            prev = cur
        return carry

    lax.fori_loop(0, LC_CH, per_channel, 0)


def _long_conv(zt, g):
    ch, _, length = zt.shape
    nb = length // LANE
    return pl.pallas_call(
        functools.partial(_lconv_kernel, nb),
        grid=(ch // LC_CH,),
        in_specs=[
            pl.BlockSpec((LC_CH, BATCH, length), lambda i: (i, 0, 0)),
            pl.BlockSpec((LC_CH, 2 * nb, LANE), lambda i: (i, 0, 0)),
        ],
        out_specs=pl.BlockSpec((LC_CH, nb * BATCH, LANE), lambda i: (i, 0, 0)),
        out_shape=jax.ShapeDtypeStruct((ch, nb * BATCH, LANE), F32),
        scratch_shapes=[pltpu.VMEM((nb * BATCH, LANE), F32)],
        compiler_params=_cparams(("arbitrary",)),
        name="hyena_long_conv",
    )(zt, g)


def _hyena_conv(z, g, seq, row_off):
    zt = z[row_off:row_off + BATCH * seq].reshape(BATCH, seq, D_HYENA).transpose(2, 0, 1)
    yt = _long_conv(zt, g)
    nb = seq // LANE
    return yt.reshape(D_HYENA, nb, BATCH, LANE).transpose(2, 1, 3, 0).reshape(BATCH * seq, D_HYENA)


def _na_bias_index():
    rows = SEQ // GRID_W
    ngrp = rows // NA_QROWS
    qcol = np.arange(GRID_W)
    cstart = np.clip(qcol - NA_WIN_COLS // 2, 0, GRID_W - NA_WIN_COLS)
    ridx = np.zeros((3, NA_QROWS, NA_KROWS), np.int32)
    valid = np.zeros((3, NA_QROWS, GRID_W, NA_KROWS, GRID_W), bool)
    for v, g in enumerate((0, 1, ngrp - 1)):
        u0 = int(np.clip(NA_QROWS * g - NA_WIN_ROWS // 2, 0, rows - NA_KROWS))
        for ri in range(NA_QROWS):
            r = NA_QROWS * g + ri
            rs = int(np.clip(r - NA_WIN_ROWS // 2, 0, rows - NA_WIN_ROWS))
            kr = u0 + np.arange(NA_KROWS)
            row_ok = (kr >= rs) & (kr < rs + NA_WIN_ROWS)
            col_ok = (qcol[None, :] >= cstart[:, None]) & (qcol[None, :] < cstart[:, None] + NA_WIN_COLS)
            ok = row_ok[None, :, None] & col_ok[:, None, :]
            valid[v, ri] = ok
            ridx[v, ri] = np.clip(kr - r + NA_WIN_ROWS - 1, 0, 2 * NA_WIN_ROWS - 2)
    rel = np.clip(qcol[None, :] - qcol[:, None] + NA_WIN_COLS - 1, 0, 2 * NA_WIN_COLS - 2)
    onehot = (rel.reshape(1, -1) == np.arange(2 * NA_WIN_COLS - 1)[:, None]).astype(np.float32)
    nq, nk = NA_QROWS * GRID_W, NA_KROWS * GRID_W
    return ridx, onehot, valid.reshape(3, nq, nk)


def _na_bias(rpb):
    ridx, onehot, valid = _na_bias_index()
    nrow = 2 * NA_WIN_ROWS - 1
    cols = jnp.dot(rpb.reshape(NA_HEADS * nrow, -1), jnp.asarray(onehot), precision=HIGHEST)
    cols = cols.reshape(NA_HEADS, nrow, GRID_W, GRID_W)
    b = cols[:, ridx]
    b = b.transpose(1, 0, 2, 4, 3, 5).reshape(3, NA_HEADS, NA_QROWS * GRID_W, NA_KROWS * GRID_W)
    return jnp.where(valid[:, None], b, NEG_INF)


def _softmax_av(s_list, v_list):
    m = functools.reduce(jnp.maximum, [s.max(axis=-1, keepdims=True) for s in s_list])
    ps = [jnp.exp(s - m) for s in s_list]
    den = functools.reduce(jnp.add, [p.sum(axis=-1, keepdims=True) for p in ps])
    o = functools.reduce(jnp.add, [jnp.dot(p.astype(BF16), v, preferred_element_type=F32) for p, v in zip(ps, v_list)])
    return o / den


def _qk(q, k):
    return lax.dot_general(q, k, (((1,), (1,)), ((), ())), preferred_element_type=F32)


def _na_kernel(q_ref, k_ref, v_ref, kc_ref, vc_ref, bias_ref, prev_ref, o_ref):
    del prev_ref
    rows = SEQ // GRID_W
    g = pl.program_id(1)
    u0 = jnp.clip(NA_QROWS * g - NA_WIN_ROWS // 2, 0, rows - NA_KROWS)
    start = pl.multiple_of(u0 * GRID_W, GRID_W)
    nk = NA_KROWS * GRID_W
    scale = NA_HEAD_DIM ** -0.5
    for h in range(NA_HEADS):
        sl = slice(h * NA_HEAD_DIM, (h + 1) * NA_HEAD_DIM)
        qh = q_ref[:, sl]
        kh = k_ref[pl.ds(start, nk), sl]
        vh = v_ref[pl.ds(start, nk), sl]
        s_loc = _qk(qh, kh) * scale + bias_ref[0, h]
        s_ctx = _qk(qh, kc_ref[:, sl]) * scale
        o_ref[:, sl] = _softmax_av([s_loc, s_ctx], [vh, vc_ref[:, sl]]).astype(o_ref.dtype)


def _na_attention(q, k, v, bias):
    nq = NA_QROWS * GRID_W
    ngrp = SEQ // nq
    ctx0 = N_LAT // CTX_LEN

    def variant(b, g):
        return (jnp.where(g == 0, 0, jnp.where(g == ngrp - 1, 2, 1)), 0, 0, 0)

    return pl.pallas_call(
        _na_kernel,
        grid=(BATCH, ngrp),
        in_specs=[
            pl.BlockSpec((nq, D_NA), lambda b, g: (b * ngrp + g, 0)),
            pl.BlockSpec((SEQ, D_NA), lambda b, g: (b, 0)),
            pl.BlockSpec((SEQ, D_NA), lambda b, g: (b, 0)),
            pl.BlockSpec((CTX_LEN, D_NA), lambda b, g: (ctx0 + b, 0)),
            pl.BlockSpec((CTX_LEN, D_NA), lambda b, g: (ctx0 + b, 0)),
            pl.BlockSpec((1, NA_HEADS, nq, NA_KROWS * GRID_W), variant),
            pl.BlockSpec(memory_space=pl.ANY),
        ],
        out_specs=pl.BlockSpec((nq, D_NA), lambda b, g: (b * ngrp + g, 0)),
        out_shape=jax.ShapeDtypeStruct((N_TOK, D_NA), BF16),
        input_output_aliases={6: 0},
        compiler_params=_cparams(("arbitrary", "arbitrary")),
        name="na_attention",
    )(q, k, v, k, v, bias, jnp.zeros((N_TOK, D_NA), BF16))


def _ctx_attn_kernel(q_ref, k_ref, v_ref, prev_ref, o_ref):
    del prev_ref
    scale = NA_HEAD_DIM ** -0.5
    for h in range(NA_HEADS):
        sl = slice(h * NA_HEAD_DIM, (h + 1) * NA_HEAD_DIM)
        s = _qk(q_ref[:, sl], k_ref[:, sl]) * scale
        o_ref[:, sl] = _softmax_av([s], [v_ref[:, sl]]).astype(o_ref.dtype)


def _ctx_attention(q, k, v, att):
    ctx0 = N_LAT // CTX_LEN
    sp = pl.BlockSpec((CTX_LEN, D_NA), lambda b: (ctx0 + b, 0))
    return pl.pallas_call(
        _ctx_attn_kernel,
        grid=(BATCH,),
        in_specs=[sp, sp, sp, pl.BlockSpec(memory_space=pl.ANY)],
        out_specs=sp,
        out_shape=jax.ShapeDtypeStruct(att.shape, att.dtype),
        input_output_aliases={3: 0},
        compiler_params=_cparams(("arbitrary",)),
        name="ctx_attention",
    )(q, k, v, att)


def _mix_kernel(yc_ref, z_ref, x0_ref, ysc_ref, ya_ref, gt_ref, xl_ref, g1_ref, sh2_ref, sc2_ref, bd_ref,
                wh_ref, ws_ref, wn_ref, wo_ref, lg_ref, lb_ref, xo_ref, ho_ref):
    dot = functools.partial(jnp.dot, preferred_element_type=F32)
    y_hy = x0_ref[...] * (yc_ref[...] + z_ref[...] * bd_ref[...])
    merged = (jax.nn.sigmoid(gt_ref[:, 0:D_MODEL]) * dot(y_hy.astype(BF16), wh_ref[...])
              + jax.nn.sigmoid(gt_ref[:, D_MODEL:2 * D_MODEL]) * dot(ysc_ref[...].astype(BF16), ws_ref[...])
              + jax.nn.sigmoid(gt_ref[:, 2 * D_MODEL:3 * D_MODEL]) * dot(ya_ref[...], wn_ref[...]))
    out = dot(merged.astype(BF16), wo_ref[...])
    xo = _layer_norm(ALPHA * xl_ref[...] + g1_ref[0] * out, lg_ref[...], lb_ref[...])
    xo_ref[...] = xo
    ho_ref[...] = xo * (1.0 + sc2_ref[0]) + sh2_ref[0]


def _mix(yc, z, x0, ysc, ya, gates, xl, g1, sh2, sc2, bias_d, wh, ws, wn, wo, ln_g, ln_b, n_tok):
    tm = TOK_TILE
    tok = lambda n: pl.BlockSpec((tm, n), lambda t: (t, 0))
    mod = pl.BlockSpec((1, 1, D_MODEL), lambda t: (_mod_row(t, SEQ // tm, N_LAT // tm), 0, 0))
    const = lambda r, n: pl.BlockSpec((r, n), lambda t: (0, 0))
    return pl.pallas_call(
        _mix_kernel,
        grid=(n_tok // tm,),
        in_specs=[tok(D_HYENA), tok(D_HYENA), tok(D_HYENA), tok(D_SCONV), tok(D_NA), tok(3 * D_MODEL), tok(D_MODEL),
                  mod, mod, mod, const(1, D_HYENA),
                  const(D_HYENA, D_MODEL), const(D_SCONV, D_MODEL), const(D_NA, D_MODEL), const(D_MODEL, D_MODEL),
                  const(1, D_MODEL), const(1, D_MODEL)],
        out_specs=[tok(D_MODEL), tok(D_MODEL)],
        out_shape=[jax.ShapeDtypeStruct((xl.shape[0], D_MODEL), F32)] * 2,
        compiler_params=_cparams(("arbitrary",)),
        name="mixer_out",
    )(yc, z, x0, ysc, ya, gates, xl, g1, sh2, sc2, bias_d, wh, ws, wn, wo, ln_g, ln_b)


def _router_kernel(h_ref, wr_ref, rb_ref, tri_ref, te_ref, rk_ref, w_ref, cnt_ref, run_ref):
    tm = h_ref.shape[0]
    per = N_EXPERTS // N_GROUPS

    @pl.when(pl.program_id(0) == 0)
    def _():
        run_ref[...] = jnp.zeros_like(run_ref)

    logits = lax.dot_general(wr_ref[...], h_ref[...], (((1,), (1,)), ((), ())),
                             precision=HIGHEST, preferred_element_type=F32)
    scores = jax.nn.sigmoid(logits)
    sel = scores + rb_ref[...]
    sub = lax.broadcasted_iota(I32, (per, tm), 0)
    colmax = lambda a: jnp.max(a, axis=0, keepdims=True)
    colmin = lambda a: jnp.min(a, axis=0, keepdims=True)
    ninf = -jnp.inf

    xs = [sel[g * per:(g + 1) * per, :] for g in range(N_GROUPS)]
    sc = [scores[g * per:(g + 1) * per, :] for g in range(N_GROUPS)]
    gs = []
    for x in xs:
        m1 = colmax(x)
        i1 = colmin(jnp.where(x == m1, sub, per))
        m2 = colmax(jnp.where(sub == i1, ninf, x))
        gs.append(m1 + m2)
    chosen = [jnp.zeros((1, tm), F32) for _ in range(N_GROUPS)]
    for _ in range(TOPK_GROUPS):
        gm = functools.reduce(jnp.maximum, gs)
        gi = jnp.full((1, tm), N_GROUPS, I32)
        for g in reversed(range(N_GROUPS)):
            gi = jnp.where(gs[g] == gm, g, gi)
        for g in range(N_GROUPS):
            hit = gi == g
            chosen[g] = jnp.where(hit, 1.0, chosen[g])
            gs[g] = jnp.where(hit, ninf, gs[g])
    xm = [jnp.where(jnp.broadcast_to(chosen[g], (per, tm)) > 0.5, xs[g], ninf) for g in range(N_GROUPS)]
    eidx = [sub + g * per for g in range(N_GROUPS)]
    picked = [jnp.zeros((per, tm), F32) for _ in range(N_GROUPS)]
    top_e, top_s = [], []
    for _ in range(TOP_K):
        em = functools.reduce(jnp.maximum, [colmax(x) for x in xm])
        ei = functools.reduce(jnp.minimum, [colmin(jnp.where(xm[g] == em, eidx[g], N_EXPERTS)) for g in range(N_GROUPS)])
        s_acc = jnp.zeros((1, tm), F32)
        for g in range(N_GROUPS):
            hit = eidx[g] == ei
            picked[g] = jnp.where(hit, 1.0, picked[g])
            xm[g] = jnp.where(hit, ninf, xm[g])
            s_acc = s_acc + jnp.sum(jnp.where(hit, sc[g], 0.0), axis=0, keepdims=True)
        top_e.append(ei)
        top_s.append(s_acc)
    den = functools.reduce(jnp.add, top_s)
    mask = jnp.concatenate(picked, axis=0)
    prefix = jnp.dot(mask.astype(BF16), tri_ref[...], preferred_element_type=F32)
    pos = run_ref[:, 0:1] + prefix
    posg = [pos[g * per:(g + 1) * per, :] for g in range(N_GROUPS)]
    for i in range(TOP_K):
        r_acc = jnp.zeros((1, tm), F32)
        for g in range(N_GROUPS):
            r_acc = r_acc + jnp.sum(jnp.where(eidx[g] == top_e[i], posg[g], 0.0), axis=0, keepdims=True)
        te_ref[0, i:i + 1, :] = top_e[i]
        rk_ref[0, i:i + 1, :] = r_acc.astype(I32)
        w_ref[0, i:i + 1, :] = top_s[i] / den * ROUTED_SCALE
    run_ref[...] = run_ref[...] + jnp.sum(mask, axis=1, keepdims=True)
    cnt_ref[...] = run_ref[...]


def _route(tokens, router_w_t, router_b, n_tok):
    tm = RT_TILE
    nt = n_tok // tm
    tri = jnp.asarray(np.triu(np.ones((tm, tm), np.float32), 1), BF16)
    out3 = lambda dt: jax.ShapeDtypeStruct((nt, TOP_K, tm), dt)
    osp = pl.BlockSpec((1, TOP_K, tm), lambda t: (t, 0, 0))
    return pl.pallas_call(
        _router_kernel,
        grid=(nt,),
        in_specs=[
            pl.BlockSpec((tm, D_MODEL), lambda t: (t, 0)),
            pl.BlockSpec((N_EXPERTS, D_MODEL), lambda t: (0, 0)),
            pl.BlockSpec((N_EXPERTS, 1), lambda t: (0, 0)),
            pl.BlockSpec((tm, tm), lambda t: (0, 0)),
        ],
        out_specs=[osp, osp, osp, pl.BlockSpec((N_EXPERTS, LANE), lambda t: (0, 0))],
        out_shape=[out3(I32), out3(I32), out3(F32), jax.ShapeDtypeStruct((N_EXPERTS, LANE), F32)],
        scratch_shapes=[pltpu.VMEM((N_EXPERTS, LANE), F32)],
        compiler_params=_cparams(("arbitrary",)),
        name="moe_router",
    )(tokens, router_w_t, router_b, tri)


def _dispatch_kernel(cnt_ref, pst_ref, na_ref, dest_ref, h_ref, xs_hbm, zero_ref, sem):
    tm = dest_ref.shape[2]
    row_copy = lambda src, dst: pltpu.make_async_copy(src, xs_hbm.at[pl.ds(dst, 1)], sem)
    n_blocks = xs_hbm.shape[0] // MOE_BLOCK

    @pl.when(pl.program_id(0) == 0)
    def _():
        zero_ref[...] = jnp.zeros_like(zero_ref)

        def blk_copy(blk):
            return pltpu.make_async_copy(zero_ref, xs_hbm.at[pl.ds(pl.multiple_of(blk * MOE_BLOCK, MOE_BLOCK), MOE_BLOCK)], sem)

        def fill_blk(blk, c):
            blk_copy(blk).start()
            return c

        def drain_blk(blk, c):
            blk_copy(blk).wait()
            return c

        lax.fori_loop(na_ref[0], n_blocks, fill_blk, 0)
        lax.fori_loop(na_ref[0], n_blocks, drain_blk, 0)

        def per_expert(e, carry):
            n = cnt_ref[e]
            n_pad = (n + MOE_BLOCK - 1) // MOE_BLOCK * MOE_BLOCK - n
            base = pst_ref[e] + n

            def fill(r, c):
                row_copy(zero_ref.at[pl.ds(0, 1)], base + r).start()
                return c

            def drain(r, c):
                row_copy(zero_ref.at[pl.ds(0, 1)], base).wait()
                return c

            lax.fori_loop(0, n_pad, fill, 0)
            lax.fori_loop(0, n_pad, drain, 0)
            return carry

        lax.fori_loop(0, N_EXPERTS, per_expert, 0)

    def issue(tb, carry):
        for tt in range(ROW_UNROLL):
            t = tb * ROW_UNROLL + tt
            for i in range(TOP_K):
                row_copy(h_ref.at[pl.ds(t, 1)], dest_ref[0, i, t]).start()
        return carry

    def drain(tb, carry):
        for _ in range(WAIT_UNROLL * TOP_K):
            row_copy(h_ref.at[pl.ds(0, 1)], 0).wait()
        return carry

    lax.fori_loop(0, tm // ROW_UNROLL, issue, 0)
    lax.fori_loop(0, tm // WAIT_UNROLL, drain, 0)


def _dispatch(counts, pstart, n_active, dest, tokens, n_slots):
    nt, _, tm = dest.shape
    grid_spec = pltpu.PrefetchScalarGridSpec(
        num_scalar_prefetch=3,
        grid=(nt,),
        in_specs=[
            pl.BlockSpec((1, TOP_K, tm), lambda t, c, p, a: (t, 0, 0), memory_space=pltpu.SMEM),
            pl.BlockSpec((tm, D_MODEL), lambda t, c, p, a: (t, 0)),
        ],
        out_specs=pl.BlockSpec(memory_space=pl.ANY),
        scratch_shapes=[pltpu.VMEM((MOE_BLOCK, D_MODEL), F32), pltpu.SemaphoreType.DMA],
    )
    return pl.pallas_call(
        _dispatch_kernel,
        grid_spec=grid_spec,
        out_shape=jax.ShapeDtypeStruct((n_slots, D_MODEL), F32),
        compiler_params=_cparams(("arbitrary",)),
        name="moe_dispatch",
    )(counts, pstart, n_active, dest, tokens)


def _expert_kernel(be_ref, na_ref, x_ref, w1_ref, w3_ref, w2_ref, y_ref):
    active = pl.program_id(0) < na_ref[0]

    @pl.when(active)
    def _():
        dot = functools.partial(jnp.dot, preferred_element_type=F32)
        x = x_ref[...].astype(BF16)
        a = dot(x, w1_ref[0])
        hidden = (a * jax.nn.sigmoid(a)) * dot(x, w3_ref[0])
        y_ref[...] = dot(hidden.astype(BF16), w2_ref[0])

    @pl.when(jnp.logical_not(active))
    def _():
        y_ref[...] = jnp.zeros_like(y_ref)


def _experts(blk_expert, n_active, xs, w1, w3, w2):
    n_blocks = xs.shape[0] // MOE_BLOCK
    xmap = lambda i, be, na: (jnp.minimum(i, na[0] - 1), 0)
    wmap = lambda i, be, na: (be[i], 0, 0)
    grid_spec = pltpu.PrefetchScalarGridSpec(
        num_scalar_prefetch=2,
        grid=(n_blocks,),
        in_specs=[
            pl.BlockSpec((MOE_BLOCK, D_MODEL), xmap),
            pl.BlockSpec((1, D_MODEL, D_EXPERT), wmap),
            pl.BlockSpec((1, D_MODEL, D_EXPERT), wmap),
            pl.BlockSpec((1, D_EXPERT, D_MODEL), wmap),
        ],
        out_specs=pl.BlockSpec((MOE_BLOCK, D_MODEL), lambda i, be, na: (i, 0)),
    )
    return pl.pallas_call(
        _expert_kernel,
        grid_spec=grid_spec,
        out_shape=jax.ShapeDtypeStruct(xs.shape, F32),
        compiler_params=_cparams(("arbitrary",)),
        name="moe_experts",
    )(blk_expert, n_active, xs, w1, w3, w2)


def _combine_kernel(dest_ref, w_ref, h_ref, xl_ref, g2_ref, s1_ref, s3_ref, s2_ref, lg_ref, lb_ref, ys_hbm,
                    xo_ref, ybuf, sem):
    tm = h_ref.shape[0]

    def row_copy(i, t, src):
        return pltpu.make_async_copy(ys_hbm.at[pl.ds(src, 1)], ybuf.at[i, pl.ds(t, 1)], sem)

    def issue(tb, carry):
        for tt in range(ROW_UNROLL):
            t = tb * ROW_UNROLL + tt
            for i in range(TOP_K):
                row_copy(i, t, dest_ref[0, i, t]).start()
        return carry

    def drain(tb, carry):
        for _ in range(WAIT_UNROLL * TOP_K):
            row_copy(0, 0, 0).wait()
        return carry

    lax.fori_loop(0, tm // ROW_UNROLL, issue, 0)
    dot = functools.partial(jnp.dot, preferred_element_type=F32)
    hb = h_ref[...].astype(BF16)
    a = dot(hb, s1_ref[...])
    f = dot(((a * jax.nn.sigmoid(a)) * dot(hb, s3_ref[...])).astype(BF16), s2_ref[...])
    lax.fori_loop(0, tm // WAIT_UNROLL, drain, 0)
    for i in range(TOP_K):
        f = f + w_ref[:, i:i + 1] * ybuf[i]
    xo_ref[...] = _layer_norm(ALPHA * xl_ref[...] + g2_ref[0] * f, lg_ref[...], lb_ref[...])


def _combine(dest, w, h, xl, g2, s1, s3, s2, ln_g, ln_b, ys, n_tok):
    tm = CB_TILE
    nt = n_tok // tm
    tok = lambda n: pl.BlockSpec((tm, n), lambda t: (t, 0))
    const = lambda r, n: pl.BlockSpec((r, n), lambda t: (0, 0))
    return pl.pallas_call(
        _combine_kernel,
        grid=(nt,),
        in_specs=[
            pl.BlockSpec((1, TOP_K, tm), lambda t: (t, 0, 0), memory_space=pltpu.SMEM),
            tok(TOP_K), tok(D_MODEL), tok(D_MODEL),
            pl.BlockSpec((1, 1, D_MODEL), lambda t: (_mod_row(t, SEQ // tm, N_LAT // tm), 0, 0)),
            const(D_MODEL, D_SHARED), const(D_MODEL, D_SHARED), const(D_SHARED, D_MODEL),
            const(1, D_MODEL), const(1, D_MODEL),
            pl.BlockSpec(memory_space=pl.ANY),
        ],
        out_specs=tok(D_MODEL),
        out_shape=jax.ShapeDtypeStruct((n_tok, D_MODEL), F32),
        scratch_shapes=[pltpu.VMEM((TOP_K, tm, D_MODEL), F32), pltpu.SemaphoreType.DMA],
        compiler_params=_cparams(("arbitrary",)),
        name="moe_combine",
    )(dest, w, h, xl, g2, s1, s3, s2, ln_g, ln_b, ys)


def _moe_layer(h, xl, g2, router_w_t, router_b, w1, w3, w2, s1, s3, s2, ln_g, ln_b, n_tok):
    top_e, rank, wts, cnt = _route(h, router_w_t, router_b, n_tok)
    counts = cnt[:, 0].astype(I32)
    padded = (counts + MOE_BLOCK - 1) // MOE_BLOCK * MOE_BLOCK
    pend = jnp.cumsum(padded)
    pstart = pend - padded
    n_blocks = n_tok * TOP_K // MOE_BLOCK + N_EXPERTS
    onehot = top_e[..., None] == jnp.arange(N_EXPERTS, dtype=I32)
    dest = rank + jnp.sum(jnp.where(onehot, pstart, 0), axis=-1)
    blk_expert = jnp.minimum(
        jnp.sum((pend // MOE_BLOCK)[None, :] <= jnp.arange(n_blocks, dtype=I32)[:, None], axis=1), N_EXPERTS - 1).astype(I32)
    n_active = (pend[-1:] // MOE_BLOCK).astype(I32)
    xs = _dispatch(counts, pstart.astype(I32), n_active, dest, h, n_blocks * MOE_BLOCK)
    ys = _experts(blk_expert, n_active, xs, w1, w3, w2)
    nt = n_tok // CB_TILE
    regroup = lambda a: a.transpose(1, 0, 2).reshape(TOP_K, n_tok)
    dest_c = regroup(dest).reshape(TOP_K, nt, CB_TILE).transpose(1, 0, 2)
    w_c = regroup(wts).T
    return _combine(dest_c, w_c, h, xl, g2, s1, s3, s2, ln_g, ln_b, ys, n_tok)


def kernel(x, c, ctx, c_ctx, w_ada, b_ada, w_in, hy_conv_w, hy_conv_b, hy_w1, hy_b1, hy_w2, hy_b2, hy_w3, hy_sin_freq, hy_bias_d, hy_proj, sc_conv_w, sc_proj, na_rpb, na_proj, w_o, ln1_g, ln1_b, ln2_g, ln2_b, moe_router, moe_bias, moe_w1, moe_w3, moe_w2, sh_w1, sh_w3, sh_w2):
    bf = lambda a: a.astype(BF16)
    xa = jnp.concatenate([x.reshape(N_LAT, D_MODEL), ctx.reshape(N_CTX, D_MODEL)], axis=0)

    cc = jnp.zeros((16, D_MODEL), F32).at[:BATCH].set(c).at[BATCH].set(c_ctx)
    mods = _ada(cc, w_ada, b_ada)[:, :BATCH + 1].reshape(DEPTH, BATCH + 1, 6, 1, D_MODEL)
    filt_lat = _hyena_filters(SEQ, hy_w1, hy_b1, hy_w2, hy_b2, hy_w3, hy_sin_freq)
    filt_ctx = _hyena_filters(CTX_LEN, hy_w1, hy_b1, hy_w2, hy_b2, hy_w3, hy_sin_freq)

    offs = [0, 3 * D_HYENA, 3 * D_HYENA + 3 * D_SCONV]
    offs += [offs[2] + D_NA, offs[2] + 2 * D_NA, offs[2] + 3 * D_NA, w_in.shape[2]]

    for i in range(DEPTH):
        last = i == DEPTH - 1
        m = [mods[i, :, j] for j in range(6)]
        w_secs = [bf(w_in[i][:, offs[j]:offs[j + 1]]) for j in range(6)]
        u_hy, u_sc, q, k, v, gates = _proj(xa, m[0], m[1], w_secs, N_TOK)

        z, x0 = _hy_pre(u_hy, hy_conv_w[i], hy_conv_b[i][None], None, SEQ, 0)
        z, x0 = _hy_pre(u_hy, hy_conv_w[i], hy_conv_b[i][None], (z, x0), CTX_LEN, N_LAT)
        yc = jnp.concatenate([_hyena_conv(z, filt_lat[i], SEQ, 0), _hyena_conv(z, filt_ctx[i], CTX_LEN, N_LAT)], axis=0)
        ysc = _short_conv(u_sc, sc_conv_w[i], None, SEQ, 0)
        ysc = _short_conv(u_sc, sc_conv_w[i], ysc, CTX_LEN, N_LAT)
        att = _na_attention(q, k, v, _na_bias(na_rpb[i]))
        att = _ctx_attention(q, k, v, att)

        xa, hmoe = _mix(yc, z, x0, ysc, att, gates, xa, m[2], m[3], m[4], hy_bias_d[i][None],
                        bf(hy_proj[i]), bf(sc_proj[i]), bf(na_proj[i]), bf(w_o[i]), ln1_g[i][None], ln1_b[i][None], N_TOK)

        n_moe = N_LAT if last else N_TOK
        xa = _moe_layer(hmoe, xa, m[5], moe_router[i].T, moe_bias[i][:, None], bf(moe_w1[i]), bf(moe_w3[i]),
                        bf(moe_w2[i]), bf(sh_w1[i]), bf(sh_w3[i]), bf(sh_w2[i]), ln2_g[i][None], ln2_b[i][None], n_moe)
    return xa.reshape(BATCH, SEQ, D_MODEL)
```

```python
import functools
import math

import numpy as np
import jax
import jax.numpy as jnp
from jax import lax
from jax.experimental import pallas as pl
from jax.experimental.pallas import tpu as pltpu

F32 = jnp.float32
BF16 = jnp.bfloat16
I32 = jnp.int32
HIGHEST = lax.Precision.HIGHEST

D_MODEL = 1024
BATCH = 8
SEQ = 4096
DEPTH = 4
CTX_LEN = 256
GRID_W = 64
D_HYENA = 512
D_SCONV = 512
NA_HEADS = 8
NA_HEAD_DIM = 64
D_NA = NA_HEADS * NA_HEAD_DIM
NA_WIN_ROWS = 8
NA_WIN_COLS = 16
HY_BANDS = 16
HY_EMB = 1 + 2 * HY_BANDS
HY_FILTER_DIM = 64
HY_FAST_DECAY = 0.3
HY_SLOW_DECAY = 1.5
HY_TARGET = 1e-2
N_EXPERTS = 64
N_GROUPS = 8
TOPK_GROUPS = 4
TOP_K = 8
D_EXPERT = 256
D_SHARED = 256
ROUTED_SCALE = 2.5
LN_EPS = 1e-5
NEG_INF = -1e30
ALPHA = (2.0 * DEPTH) ** 0.25

N_LAT = BATCH * SEQ
N_CTX = BATCH * CTX_LEN
N_TOK = N_LAT + N_CTX

LANE = 128
SUBLANE = 8
VMEM_LIMIT = 56 * 1024 * 1024

TOK_TILE = 256
NA_QROWS = 4
NA_KROWS = NA_QROWS + NA_WIN_ROWS - 1
MOE_BLOCK = 256
RT_TILE = 512
CB_TILE = 128
LC_CH = 8
ROW_UNROLL = 4
WAIT_UNROLL = 16
TOK_ROWS = D_MODEL // LANE
assert TOK_ROWS == SUBLANE


def _cparams(sem):
    return pltpu.CompilerParams(dimension_semantics=sem, vmem_limit_bytes=VMEM_LIMIT)


def _mod_row(tile, tiles_per_batch, n_lat_tiles):
    return jnp.where(tile < n_lat_tiles, tile // tiles_per_batch, BATCH)


def _layer_norm(v, g, b):
    mu = jnp.mean(v, axis=-1, keepdims=True)
    c = v - mu
    var = jnp.mean(c * c, axis=-1, keepdims=True)
    return c * lax.rsqrt(var + LN_EPS) * g + b


def _ada_kernel(c_ref, w_ref, b_ref, o_ref):
    c = c_ref[...]
    cond = c * jax.nn.sigmoid(c)
    o_ref[0] = jnp.dot(cond, w_ref[0], precision=HIGHEST, preferred_element_type=F32) + b_ref[0]


def _ada(cc, w_ada, b_ada):
    depth, d, n = w_ada.shape
    tn = 1536
    return pl.pallas_call(
        _ada_kernel,
        grid=(depth, n // tn),
        in_specs=[
            pl.BlockSpec((16, d), lambda l, j: (0, 0)),
            pl.BlockSpec((1, d, tn), lambda l, j: (l, 0, j)),
            pl.BlockSpec((1, 1, tn), lambda l, j: (l, 0, j)),
        ],
        out_specs=pl.BlockSpec((1, 16, tn), lambda l, j: (l, 0, j)),
        out_shape=jax.ShapeDtypeStruct((depth, 16, n), F32),
        compiler_params=_cparams(("arbitrary", "arbitrary")),
        name="ada",
    )(cc, w_ada, b_ada.reshape(depth, 1, n))


def _proj_kernel(x_ref, sh_ref, sc_ref, w_hy, w_sc, w_q, w_k, w_v, w_g, o_hy, o_sc, o_q, o_k, o_v, o_g):
    h = (x_ref[...] * (1.0 + sc_ref[0]) + sh_ref[0]).astype(BF16)
    for w, o in ((w_hy, o_hy), (w_sc, o_sc), (w_q, o_q), (w_k, o_k), (w_v, o_v), (w_g, o_g)):
        o[...] = jnp.dot(h, w[...], preferred_element_type=F32).astype(o.dtype)


def _proj(x, shift, scale, ws, n_tok):
    tm = TOK_TILE
    nt = n_tok // tm
    mod = lambda t: (_mod_row(t, SEQ // tm, N_LAT // tm), 0, 0)
    widths = [w.shape[1] for w in ws]
    dtypes = [F32, F32, BF16, BF16, BF16, F32]
    return pl.pallas_call(
        _proj_kernel,
        grid=(nt,),
        in_specs=[
            pl.BlockSpec((tm, D_MODEL), lambda t: (t, 0)),
            pl.BlockSpec((1, 1, D_MODEL), mod),
            pl.BlockSpec((1, 1, D_MODEL), mod),
        ] + [pl.BlockSpec((D_MODEL, n), lambda t: (0, 0), pipeline_mode=pl.Buffered(1)) for n in widths],
        out_specs=[pl.BlockSpec((tm, n), lambda t: (t, 0)) for n in widths],
        out_shape=[jax.ShapeDtypeStruct((x.shape[0], n), dt) for n, dt in zip(widths, dtypes)],
        compiler_params=_cparams(("arbitrary",)),
        name="in_proj",
    )(x, shift, scale, *ws)


def _dwconv3(u, w):
    s = u.shape[0]
    row = lax.broadcasted_iota(I32, u.shape, 0)
    prev = jnp.where(row == 0, 0.0, pltpu.roll(u, 1, 0))
    nxt = jnp.where(row == s - 1, 0.0, pltpu.roll(u, s - 1, 0))
    return prev * w[0:1] + u * w[1:2] + nxt * w[2:3]


def _hy_pre_kernel(u0_ref, u1_ref, u2_ref, w0_ref, w1_ref, w2_ref, b0_ref, b1_ref, b2_ref, pz_ref, px_ref,
                   z_ref, x0_ref):
    del pz_ref, px_ref
    x0_ref[...] = _dwconv3(u0_ref[...], w0_ref[...]) + b0_ref[...]
    x1 = _dwconv3(u1_ref[...], w1_ref[...]) + b1_ref[...]
    v = _dwconv3(u2_ref[...], w2_ref[...]) + b2_ref[...]
    z_ref[...] = v * x1


def _hy_pre(u, conv_w, conv_b, prev, seq, row_off):
    ncb = D_HYENA // LANE
    ob = row_off // seq
    usp = lambda s: pl.BlockSpec((seq, LANE), lambda b, c: (ob + b, s * ncb + c))
    wsp = lambda s: pl.BlockSpec((3, LANE), lambda b, c: (0, s * ncb + c))
    bsp = lambda s: pl.BlockSpec((1, LANE), lambda b, c: (0, s * ncb + c))
    osp = pl.BlockSpec((seq, LANE), lambda b, c: (ob + b, c))
    n_tok = u.shape[0]
    if prev is None:
        prev = (jnp.zeros((n_tok, D_HYENA), F32), jnp.zeros((n_tok, D_HYENA), F32))
    args = [u, u, u, conv_w, conv_w, conv_w, conv_b, conv_b, conv_b] + list(prev)
    in_specs = [usp(0), usp(1), usp(2), wsp(0), wsp(1), wsp(2), bsp(0), bsp(1), bsp(2)]
    in_specs += [pl.BlockSpec(memory_space=pl.ANY)] * 2
    return pl.pallas_call(
        _hy_pre_kernel,
        grid=(BATCH, ncb),
        in_specs=in_specs,
        out_specs=[osp, osp],
        out_shape=[jax.ShapeDtypeStruct((n_tok, D_HYENA), F32)] * 2,
        input_output_aliases={9: 0, 10: 1},
        compiler_params=_cparams(("arbitrary", "arbitrary")),
        name="hyena_pre",
    )(*args)


def _sc_kernel(bg_ref, cg_ref, xs_ref, w_ref, prev_ref, o_ref):
    del prev_ref
    o_ref[...] = bg_ref[...] * _dwconv3(cg_ref[...] * xs_ref[...], w_ref[...])


def _short_conv(u, conv_w, prev, seq, row_off):
    ncb = D_SCONV // LANE
    ob = row_off // seq
    usp = lambda s: pl.BlockSpec((seq, LANE), lambda b, c: (ob + b, s * ncb + c))
    osp = pl.BlockSpec((seq, LANE), lambda b, c: (ob + b, c))
    if prev is None:
        prev = jnp.zeros((u.shape[0], D_SCONV), F32)
    in_specs = [usp(0), usp(1), usp(2), pl.BlockSpec((3, LANE), lambda b, c: (0, c)),
                pl.BlockSpec(memory_space=pl.ANY)]
    return pl.pallas_call(
        _sc_kernel,
        grid=(BATCH, ncb),
        in_specs=in_specs,
        out_specs=osp,
        out_shape=jax.ShapeDtypeStruct((u.shape[0], D_SCONV), F32),
        input_output_aliases={4: 0},
        compiler_params=_cparams(("arbitrary", "arbitrary")),
        name="short_conv",
    )(u, u, u, conv_w, prev)


def _filter_kernel(f_ref, dec_ref, w1_ref, b1_ref, w2_ref, b2_ref, w3_ref, fr_ref, o_ref):
    dot = functools.partial(jnp.dot, precision=HIGHEST, preferred_element_type=F32)
    z = jnp.sin(fr_ref[0, 0:1] * (dot(f_ref[...], w1_ref[0]) + b1_ref[0]))
    z = jnp.sin(fr_ref[0, 1:2] * (dot(z, w2_ref[0]) + b2_ref[0]))
    o_ref[0] = dot(z, w3_ref[0]) * dec_ref[...]


def _filter_consts(length):
    t = jnp.linspace(0.0, 1.0, length, dtype=F32)[:, None]
    ang = (2.0 * math.pi / length) * jnp.arange(length, dtype=F32)[:, None]
    bands = jnp.linspace(1e-4, HY_BANDS - 1, HY_BANDS, dtype=F32)[None, :]
    feats = jnp.concatenate([t, jnp.cos(bands * ang), -jnp.sin(bands * ang)], -1)
    feats = jnp.pad(feats, ((0, 0), (0, LANE - HY_EMB)))
    deltas = jnp.abs(jnp.linspace(math.log(HY_TARGET) / HY_SLOW_DECAY, math.log(HY_TARGET) / HY_FAST_DECAY,
                                  D_HYENA, dtype=F32))
    decay = jnp.exp(-t * deltas[None, :])
    return feats, jnp.concatenate([decay, decay], -1)


def _hyena_filters(length, w1, b1, w2, b2, w3, sin_freq):
    depth = w1.shape[0]
    pf = LANE - HY_FILTER_DIM
    w1p = jnp.pad(w1, ((0, 0), (0, LANE - HY_EMB), (0, pf)))
    b1p = jnp.pad(b1, ((0, 0), (0, pf)))[:, None, :]
    w2p = jnp.pad(w2, ((0, 0), (0, pf), (0, pf)))
    b2p = jnp.pad(b2, ((0, 0), (0, pf)))[:, None, :]
    w3p = jnp.pad(w3, ((0, 0), (0, pf), (0, 0)))
    frp = jnp.pad(sin_freq, ((0, 0), (0, 0), (0, pf)))
    feats, decay = _filter_consts(length)
    tl = min(length, 512)
    lsp = lambda shape: pl.BlockSpec((1,) + shape, lambda l, j: (l, 0, 0))
    h = pl.pallas_call(
        _filter_kernel,
        grid=(depth, length // tl),
        in_specs=[
            pl.BlockSpec((tl, LANE), lambda l, j: (j, 0)),
            pl.BlockSpec((tl, 2 * D_HYENA), lambda l, j: (j, 0)),
            lsp((LANE, LANE)), lsp((1, LANE)), lsp((LANE, LANE)), lsp((1, LANE)),
            lsp((LANE, 2 * D_HYENA)), lsp((2, LANE)),
        ],
        out_specs=pl.BlockSpec((1, tl, 2 * D_HYENA), lambda l, j: (l, j, 0)),
        out_shape=jax.ShapeDtypeStruct((depth, length, 2 * D_HYENA), F32),
        compiler_params=_cparams(("arbitrary", "arbitrary")),
        name="hyena_filter",
    )(feats, decay, w1p, b1p, w2p, b2p, w3p, frp)
    h_fwd = h[:, :, :D_HYENA]
    h_bwd = h[:, :, D_HYENA:]
    g_lin = jnp.concatenate([jnp.zeros((depth, 1, D_HYENA), F32), h_bwd[:, :0:-1], h_fwd], axis=1)
    return g_lin.transpose(0, 2, 1).reshape(depth, D_HYENA, 2 * length // LANE, LANE)


def _lconv_kernel(nb, z_ref, g_ref, y_ref, zl_ref):
    krow = lax.broadcasted_iota(I32, (LANE, LANE), 0)
    acol = lax.broadcasted_iota(I32, (LANE, LANE), 1)
    upper = acol >= krow

    def shifted(c, seg):
        return pltpu.roll(jnp.broadcast_to(g_ref[c, seg:seg + 1, :], (LANE, LANE)), 0, 1, stride=1, stride_axis=0)

    def per_channel(c, carry):
        for j in range(nb):
            zl_ref[j * BATCH:(j + 1) * BATCH, :] = z_ref[c, :, j * LANE:(j + 1) * LANE]
        y_ref[c] = jnp.zeros((nb * BATCH, LANE), F32)
        prev = shifted(c, 0)
        for d in range(1 - nb, nb):
            cur = shifted(c, d + nb)
            toep = jnp.where(upper, cur, prev).astype(BF16)
            j0, j1 = max(0, -d), min(nb, nb - d)
            part = jnp.dot(zl_ref[j0 * BATCH:j1 * BATCH, :].astype(BF16), toep, preferred_element_type=F32)
            y_ref[c, (j0 + d) * BATCH:(j1 + d) * BATCH, :] += part
            prev = cur
        return carry

    lax.fori_loop(0, LC_CH, per_channel, 0)


def _long_conv(zt, g):
    ch, _, length = zt.shape
    nb = length // LANE
    return pl.pallas_call(
        functools.partial(_lconv_kernel, nb),
        grid=(ch // LC_CH,),
        in_specs=[
            pl.BlockSpec((LC_CH, BATCH, length), lambda i: (i, 0, 0)),
            pl.BlockSpec((LC_CH, 2 * nb, LANE), lambda i: (i, 0, 0)),
        ],
        out_specs=pl.BlockSpec((LC_CH, nb * BATCH, LANE), lambda i: (i, 0, 0)),
        out_shape=jax.ShapeDtypeStruct((ch, nb * BATCH, LANE), F32),
        scratch_shapes=[pltpu.VMEM((nb * BATCH, LANE), F32)],
        compiler_params=_cparams(("arbitrary",)),
        name="hyena_long_conv",
    )(zt, g)


def _hyena_conv(z, g, seq, row_off):
    zt = z[row_off:row_off + BATCH * seq].reshape(BATCH, seq, D_HYENA).transpose(2, 0, 1)
    yt = _long_conv(zt, g)
    nb = seq // LANE
    return yt.reshape(D_HYENA, nb, BATCH, LANE).transpose(2, 1, 3, 0).reshape(BATCH * seq, D_HYENA)


def _na_bias_index():
    rows = SEQ // GRID_W
    ngrp = rows // NA_QROWS
    qcol = np.arange(GRID_W)
    cstart = np.clip(qcol - NA_WIN_COLS // 2, 0, GRID_W - NA_WIN_COLS)
    ridx = np.zeros((3, NA_QROWS, NA_KROWS), np.int32)
    valid = np.zeros((3, NA_QROWS, GRID_W, NA_KROWS, GRID_W), bool)
    for v, g in enumerate((0, 1, ngrp - 1)):
        u0 = int(np.clip(NA_QROWS * g - NA_WIN_ROWS // 2, 0, rows - NA_KROWS))
        for ri in range(NA_QROWS):
            r = NA_QROWS * g + ri
            rs = int(np.clip(r - NA_WIN_ROWS // 2, 0, rows - NA_WIN_ROWS))
            kr = u0 + np.arange(NA_KROWS)
            row_ok = (kr >= rs) & (kr < rs + NA_WIN_ROWS)
            col_ok = (qcol[None, :] >= cstart[:, None]) & (qcol[None, :] < cstart[:, None] + NA_WIN_COLS)
            ok = row_ok[None, :, None] & col_ok[:, None, :]
            valid[v, ri] = ok
            ridx[v, ri] = np.clip(kr - r + NA_WIN_ROWS - 1, 0, 2 * NA_WIN_ROWS - 2)
    rel = np.clip(qcol[None, :] - qcol[:, None] + NA_WIN_COLS - 1, 0, 2 * NA_WIN_COLS - 2)
    onehot = (rel.reshape(1, -1) == np.arange(2 * NA_WIN_COLS - 1)[:, None]).astype(np.float32)
    nq, nk = NA_QROWS * GRID_W, NA_KROWS * GRID_W
    return ridx, onehot, valid.reshape(3, nq, nk)


def _na_bias(rpb):
    ridx, onehot, valid = _na_bias_index()
    nrow = 2 * NA_WIN_ROWS - 1
    cols = jnp.dot(rpb.reshape(NA_HEADS * nrow, -1), jnp.asarray(onehot), precision=HIGHEST)
    cols = cols.reshape(NA_HEADS, nrow, GRID_W, GRID_W)
    b = cols[:, ridx]
    b = b.transpose(1, 0, 2, 4, 3, 5).reshape(3, NA_HEADS, NA_QROWS * GRID_W, NA_KROWS * GRID_W)
    return jnp.where(valid[:, None], b, NEG_INF)


def _softmax_av(s_list, v_list):
    m = functools.reduce(jnp.maximum, [s.max(axis=-1, keepdims=True) for s in s_list])
    ps = [jnp.exp(s - m) for s in s_list]
    den = functools.reduce(jnp.add, [p.sum(axis=-1, keepdims=True) for p in ps])
    o = functools.reduce(jnp.add, [jnp.dot(p.astype(BF16), v, preferred_element_type=F32) for p, v in zip(ps, v_list)])
    return o / den


def _qk(q, k):
    return lax.dot_general(q, k, (((1,), (1,)), ((), ())), preferred_element_type=F32)


def _na_kernel(q_ref, k_ref, v_ref, kc_ref, vc_ref, bias_ref, prev_ref, o_ref):
    del prev_ref
    rows = SEQ // GRID_W
    g = pl.program_id(1)
    u0 = jnp.clip(NA_QROWS * g - NA_WIN_ROWS // 2, 0, rows - NA_KROWS)
    start = pl.multiple_of(u0 * GRID_W, GRID_W)
    nk = NA_KROWS * GRID_W
    scale = NA_HEAD_DIM ** -0.5
    for h in range(NA_HEADS):
        sl = slice(h * NA_HEAD_DIM, (h + 1) * NA_HEAD_DIM)
        qh = q_ref[:, sl]
        kh = k_ref[pl.ds(start, nk), sl]
        vh = v_ref[pl.ds(start, nk), sl]
        s_loc = _qk(qh, kh) * scale + bias_ref[0, h]
        s_ctx = _qk(qh, kc_ref[:, sl]) * scale
        o_ref[:, sl] = _softmax_av([s_loc, s_ctx], [vh, vc_ref[:, sl]]).astype(o_ref.dtype)


def _na_attention(q, k, v, bias):
    nq = NA_QROWS * GRID_W
    ngrp = SEQ // nq
    ctx0 = N_LAT // CTX_LEN

    def variant(b, g):
        return (jnp.where(g == 0, 0, jnp.where(g == ngrp - 1, 2, 1)), 0, 0, 0)

    return pl.pallas_call(
        _na_kernel,
        grid=(BATCH, ngrp),
        in_specs=[
            pl.BlockSpec((nq, D_NA), lambda b, g: (b * ngrp + g, 0)),
            pl.BlockSpec((SEQ, D_NA), lambda b, g: (b, 0)),
            pl.BlockSpec((SEQ, D_NA), lambda b, g: (b, 0)),
            pl.BlockSpec((CTX_LEN, D_NA), lambda b, g: (ctx0 + b, 0)),
            pl.BlockSpec((CTX_LEN, D_NA), lambda b, g: (ctx0 + b, 0)),
            pl.BlockSpec((1, NA_HEADS, nq, NA_KROWS * GRID_W), variant),
            pl.BlockSpec(memory_space=pl.ANY),
        ],
        out_specs=pl.BlockSpec((nq, D_NA), lambda b, g: (b * ngrp + g, 0)),
        out_shape=jax.ShapeDtypeStruct((N_TOK, D_NA), BF16),
        input_output_aliases={6: 0},
        compiler_params=_cparams(("arbitrary", "arbitrary")),
        name="na_attention",
    )(q, k, v, k, v, bias, jnp.zeros((N_TOK, D_NA), BF16))


def _ctx_attn_kernel(q_ref, k_ref, v_ref, prev_ref, o_ref):
    del prev_ref
    scale = NA_HEAD_DIM ** -0.5
    for h in range(NA_HEADS):
        sl = slice(h * NA_HEAD_DIM, (h + 1) * NA_HEAD_DIM)
        s = _qk(q_ref[:, sl], k_ref[:, sl]) * scale
        o_ref[:, sl] = _softmax_av([s], [v_ref[:, sl]]).astype(o_ref.dtype)


def _ctx_attention(q, k, v, att):
    ctx0 = N_LAT // CTX_LEN
    sp = pl.BlockSpec((CTX_LEN, D_NA), lambda b: (ctx0 + b, 0))
    return pl.pallas_call(
        _ctx_attn_kernel,
        grid=(BATCH,),
        in_specs=[sp, sp, sp, pl.BlockSpec(memory_space=pl.ANY)],
        out_specs=sp,
        out_shape=jax.ShapeDtypeStruct(att.shape, att.dtype),
        input_output_aliases={3: 0},
        compiler_params=_cparams(("arbitrary",)),
        name="ctx_attention",
    )(q, k, v, att)


def _mix_kernel(yc_ref, z_ref, x0_ref, ysc_ref, ya_ref, gt_ref, xl_ref, g1_ref, sh2_ref, sc2_ref, bd_ref,
                wh_ref, ws_ref, wn_ref, wo_ref, lg_ref, lb_ref, xo_ref, ho_ref):
    dot = functools.partial(jnp.dot, preferred_element_type=F32)
    y_hy = x0_ref[...] * (yc_ref[...] + z_ref[...] * bd_ref[...])
    merged = (jax.nn.sigmoid(gt_ref[:, 0:D_MODEL]) * dot(y_hy.astype(BF16), wh_ref[...])
              + jax.nn.sigmoid(gt_ref[:, D_MODEL:2 * D_MODEL]) * dot(ysc_ref[...].astype(BF16), ws_ref[...])
              + jax.nn.sigmoid(gt_ref[:, 2 * D_MODEL:3 * D_MODEL]) * dot(ya_ref[...], wn_ref[...]))
    out = dot(merged.astype(BF16), wo_ref[...])
    xo = _layer_norm(ALPHA * xl_ref[...] + g1_ref[0] * out, lg_ref[...], lb_ref[...])
    xo_ref[...] = xo
    ho_ref[...] = xo * (1.0 + sc2_ref[0]) + sh2_ref[0]


def _mix(yc, z, x0, ysc, ya, gates, xl, g1, sh2, sc2, bias_d, wh, ws, wn, wo, ln_g, ln_b, n_tok):
    tm = TOK_TILE
    tok = lambda n: pl.BlockSpec((tm, n), lambda t: (t, 0))
    mod = pl.BlockSpec((1, 1, D_MODEL), lambda t: (_mod_row(t, SEQ // tm, N_LAT // tm), 0, 0))
    const = lambda r, n: pl.BlockSpec((r, n), lambda t: (0, 0))
    return pl.pallas_call(
        _mix_kernel,
        grid=(n_tok // tm,),
        in_specs=[tok(D_HYENA), tok(D_HYENA), tok(D_HYENA), tok(D_SCONV), tok(D_NA), tok(3 * D_MODEL), tok(D_MODEL),
                  mod, mod, mod, const(1, D_HYENA),
                  const(D_HYENA, D_MODEL), const(D_SCONV, D_MODEL), const(D_NA, D_MODEL), const(D_MODEL, D_MODEL),
                  const(1, D_MODEL), const(1, D_MODEL)],
        out_specs=[tok(D_MODEL), tok(D_MODEL)],
        out_shape=[jax.ShapeDtypeStruct((xl.shape[0], D_MODEL), F32)] * 2,
        compiler_params=_cparams(("arbitrary",)),
        name="mixer_out",
    )(yc, z, x0, ysc, ya, gates, xl, g1, sh2, sc2, bias_d, wh, ws, wn, wo, ln_g, ln_b)


def _router_kernel(h_ref, wr_ref, rb_ref, tri_ref, te_ref, rk_ref, w_ref, cnt_ref, run_ref):
    tm = h_ref.shape[0]
    per = N_EXPERTS // N_GROUPS

    @pl.when(pl.program_id(0) == 0)
    def _():
        run_ref[...] = jnp.zeros_like(run_ref)

    logits = lax.dot_general(wr_ref[...], h_ref[...], (((1,), (1,)), ((), ())),
                             precision=HIGHEST, preferred_element_type=F32)
    scores = jax.nn.sigmoid(logits)
    sel = scores + rb_ref[...]
    sub = lax.broadcasted_iota(I32, (per, tm), 0)
    colmax = lambda a: jnp.max(a, axis=0, keepdims=True)
    colmin = lambda a: jnp.min(a, axis=0, keepdims=True)
    ninf = -jnp.inf

    xs = [sel[g * per:(g + 1) * per, :] for g in range(N_GROUPS)]
    sc = [scores[g * per:(g + 1) * per, :] for g in range(N_GROUPS)]
    gs = []
    for x in xs:
        m1 = colmax(x)
        i1 = colmin(jnp.where(x == m1, sub, per))
        m2 = colmax(jnp.where(sub == i1, ninf, x))
        gs.append(m1 + m2)
    chosen = [jnp.zeros((1, tm), F32) for _ in range(N_GROUPS)]
    for _ in range(TOPK_GROUPS):
        gm = functools.reduce(jnp.maximum, gs)
        gi = jnp.full((1, tm), N_GROUPS, I32)
        for g in reversed(range(N_GROUPS)):
            gi = jnp.where(gs[g] == gm, g, gi)
        for g in range(N_GROUPS):
            hit = gi == g
            chosen[g] = jnp.where(hit, 1.0, chosen[g])
            gs[g] = jnp.where(hit, ninf, gs[g])
    xm = [jnp.where(jnp.broadcast_to(chosen[g], (per, tm)) > 0.5, xs[g], ninf) for g in range(N_GROUPS)]
    eidx = [sub + g * per for g in range(N_GROUPS)]
    picked = [jnp.zeros((per, tm), F32) for _ in range(N_GROUPS)]
    top_e, top_s = [], []
    for _ in range(TOP_K):
        em = functools.reduce(jnp.maximum, [colmax(x) for x in xm])
        ei = functools.reduce(jnp.minimum, [colmin(jnp.where(xm[g] == em, eidx[g], N_EXPERTS)) for g in range(N_GROUPS)])
        s_acc = jnp.zeros((1, tm), F32)
        for g in range(N_GROUPS):
            hit = eidx[g] == ei
            picked[g] = jnp.where(hit, 1.0, picked[g])
            xm[g] = jnp.where(hit, ninf, xm[g])
            s_acc = s_acc + jnp.sum(jnp.where(hit, sc[g], 0.0), axis=0, keepdims=True)
        top_e.append(ei)
        top_s.append(s_acc)
    den = functools.reduce(jnp.add, top_s)
    mask = jnp.concatenate(picked, axis=0)
    prefix = jnp.dot(mask.astype(BF16), tri_ref[...], preferred_element_type=F32)
    pos = run_ref[:, 0:1] + prefix
    posg = [pos[g * per:(g + 1) * per, :] for g in range(N_GROUPS)]
    for i in range(TOP_K):
        r_acc = jnp.zeros((1, tm), F32)
        for g in range(N_GROUPS):
            r_acc = r_acc + jnp.sum(jnp.where(eidx[g] == top_e[i], posg[g], 0.0), axis=0, keepdims=True)
        te_ref[0, i:i + 1, :] = top_e[i]
        rk_ref[0, i:i + 1, :] = r_acc.astype(I32)
        w_ref[0, i:i + 1, :] = top_s[i] / den * ROUTED_SCALE
    run_ref[...] = run_ref[...] + jnp.sum(mask, axis=1, keepdims=True)
    cnt_ref[...] = run_ref[...]


def _route(tokens, router_w_t, router_b, n_tok):
    tm = RT_TILE
    nt = n_tok // tm
    tri = jnp.asarray(np.triu(np.ones((tm, tm), np.float32), 1), BF16)
    out3 = lambda dt: jax.ShapeDtypeStruct((nt, TOP_K, tm), dt)
    osp = pl.BlockSpec((1, TOP_K, tm), lambda t: (t, 0, 0))
    return pl.pallas_call(
        _router_kernel,
        grid=(nt,),
        in_specs=[
            pl.BlockSpec((tm, D_MODEL), lambda t: (t, 0)),
            pl.BlockSpec((N_EXPERTS, D_MODEL), lambda t: (0, 0)),
            pl.BlockSpec((N_EXPERTS, 1), lambda t: (0, 0)),
            pl.BlockSpec((tm, tm), lambda t: (0, 0)),
        ],
        out_specs=[osp, osp, osp, pl.BlockSpec((N_EXPERTS, LANE), lambda t: (0, 0))],
        out_shape=[out3(I32), out3(I32), out3(F32), jax.ShapeDtypeStruct((N_EXPERTS, LANE), F32)],
        scratch_shapes=[pltpu.VMEM((N_EXPERTS, LANE), F32)],
        compiler_params=_cparams(("arbitrary",)),
        name="moe_router",
    )(tokens, router_w_t, router_b, tri)


def _tok_rows(t):
    return pl.ds(pl.multiple_of(t * TOK_ROWS, TOK_ROWS), TOK_ROWS)


def _dispatch_kernel(cnt_ref, pst_ref, na_ref, dest_ref, h_ref, xs_hbm, zero_ref, sem):
    tm = dest_ref.shape[2]
    row_copy = lambda src, dst: pltpu.make_async_copy(src, xs_hbm.at[_tok_rows(dst)], sem)
    blk_rows = MOE_BLOCK * TOK_ROWS
    n_blocks = xs_hbm.shape[0] // blk_rows

    @pl.when(pl.program_id(0) == 0)
    def _():
        zero_ref[...] = jnp.zeros_like(zero_ref)

        def blk_copy(blk):
            return pltpu.make_async_copy(zero_ref, xs_hbm.at[pl.ds(pl.multiple_of(blk * blk_rows, blk_rows), blk_rows)], sem)

        def fill_blk(blk, c):
            blk_copy(blk).start()
            return c

        def drain_blk(blk, c):
            blk_copy(blk).wait()
            return c

        lax.fori_loop(na_ref[0], n_blocks, fill_blk, 0)
        lax.fori_loop(na_ref[0], n_blocks, drain_blk, 0)

        def per_expert(e, carry):
            n = cnt_ref[e]
            n_pad = (n + MOE_BLOCK - 1) // MOE_BLOCK * MOE_BLOCK - n
            base = pst_ref[e] + n

            def fill(r, c):
                row_copy(zero_ref.at[_tok_rows(0)], base + r).start()
                return c

            def drain(r, c):
                row_copy(zero_ref.at[_tok_rows(0)], base).wait()
                return c

            lax.fori_loop(0, n_pad, fill, 0)
            lax.fori_loop(0, n_pad, drain, 0)
            return carry

        lax.fori_loop(0, N_EXPERTS, per_expert, 0)

    def issue(tb, carry):
        for tt in range(ROW_UNROLL):
            t = tb * ROW_UNROLL + tt
            for i in range(TOP_K):
                row_copy(h_ref.at[_tok_rows(t)], dest_ref[0, i, t]).start()
        return carry

    def drain(tb, carry):
        for _ in range(WAIT_UNROLL * TOP_K):
            row_copy(h_ref.at[_tok_rows(0)], 0).wait()
        return carry

    lax.fori_loop(0, tm // ROW_UNROLL, issue, 0)
    lax.fori_loop(0, tm // WAIT_UNROLL, drain, 0)


def _dispatch(counts, pstart, n_active, dest, tokens, n_slots):
    nt, _, tm = dest.shape
    grid_spec = pltpu.PrefetchScalarGridSpec(
        num_scalar_prefetch=3,
        grid=(nt,),
        in_specs=[
            pl.BlockSpec((1, TOP_K, tm), lambda t, c, p, a: (t, 0, 0), memory_space=pltpu.SMEM),
            pl.BlockSpec((tm * TOK_ROWS, LANE), lambda t, c, p, a: (t, 0)),
        ],
        out_specs=pl.BlockSpec(memory_space=pl.ANY),
        scratch_shapes=[pltpu.VMEM((MOE_BLOCK * TOK_ROWS, LANE), F32), pltpu.SemaphoreType.DMA],
    )
    return pl.pallas_call(
        _dispatch_kernel,
        grid_spec=grid_spec,
        out_shape=jax.ShapeDtypeStruct((n_slots * TOK_ROWS, LANE), F32),
        compiler_params=_cparams(("arbitrary",)),
        name="moe_dispatch",
    )(counts, pstart, n_active, dest, tokens)


def _expert_kernel(be_ref, na_ref, x_ref, w1_ref, w3_ref, w2_ref, y_ref):
    active = pl.program_id(0) < na_ref[0]

    @pl.when(active)
    def _():
        dot = functools.partial(jnp.dot, preferred_element_type=F32)
        x = jnp.concatenate([x_ref[pl.ds(j, MOE_BLOCK, stride=TOK_ROWS), :] for j in range(TOK_ROWS)], axis=-1)
        x = x.astype(BF16)
        a = dot(x, w1_ref[0])
        hidden = (a * jax.nn.sigmoid(a)) * dot(x, w3_ref[0])
        y = dot(hidden.astype(BF16), w2_ref[0])
        for j in range(TOK_ROWS):
            y_ref[pl.ds(j, MOE_BLOCK, stride=TOK_ROWS), :] = y[:, j * LANE:(j + 1) * LANE]

    @pl.when(jnp.logical_not(active))
    def _():
        y_ref[...] = jnp.zeros_like(y_ref)


def _experts(blk_expert, n_active, xs, w1, w3, w2):
    blk_rows = MOE_BLOCK * TOK_ROWS
    n_blocks = xs.shape[0] // blk_rows
    xmap = lambda i, be, na: (jnp.minimum(i, na[0] - 1), 0)
    wmap = lambda i, be, na: (be[i], 0, 0)
    grid_spec = pltpu.PrefetchScalarGridSpec(
        num_scalar_prefetch=2,
        grid=(n_blocks,),
        in_specs=[
            pl.BlockSpec((blk_rows, LANE), xmap),
            pl.BlockSpec((1, D_MODEL, D_EXPERT), wmap),
            pl.BlockSpec((1, D_MODEL, D_EXPERT), wmap),
            pl.BlockSpec((1, D_EXPERT, D_MODEL), wmap),
        ],
        out_specs=pl.BlockSpec((blk_rows, LANE), lambda i, be, na: (i, 0)),
    )
    return pl.pallas_call(
        _expert_kernel,
        grid_spec=grid_spec,
        out_shape=jax.ShapeDtypeStruct(xs.shape, F32),
        compiler_params=_cparams(("arbitrary",)),
        name="moe_experts",
    )(blk_expert, n_active, xs, w1, w3, w2)


def _combine_kernel(dest_ref, w_ref, h_ref, xl_ref, g2_ref, s1_ref, s3_ref, s2_ref, lg_ref, lb_ref, ys_hbm,
                    xo_ref, ybuf, sem):
    tm = h_ref.shape[0]

    def row_copy(i, t, src):
        return pltpu.make_async_copy(ys_hbm.at[_tok_rows(src)], ybuf.at[i, _tok_rows(t)], sem)

    def issue(tb, carry):
        for tt in range(ROW_UNROLL):
            t = tb * ROW_UNROLL + tt
            for i in range(TOP_K):
                row_copy(i, t, dest_ref[0, i, t]).start()
        return carry

    def drain(tb, carry):
        for _ in range(WAIT_UNROLL * TOP_K):
            row_copy(0, 0, 0).wait()
        return carry

    lax.fori_loop(0, tm // ROW_UNROLL, issue, 0)
    dot = functools.partial(jnp.dot, preferred_element_type=F32)
    hb = h_ref[...].astype(BF16)
    a = dot(hb, s1_ref[...])
    f = dot(((a * jax.nn.sigmoid(a)) * dot(hb, s3_ref[...])).astype(BF16), s2_ref[...])
    lax.fori_loop(0, tm // WAIT_UNROLL, drain, 0)
    routed = []
    for j in range(TOK_ROWS):
        routed.append(functools.reduce(jnp.add, [w_ref[:, i:i + 1] * ybuf[i, pl.ds(j, tm, stride=TOK_ROWS), :]
                                                 for i in range(TOP_K)]))
    f = f + jnp.concatenate(routed, axis=-1)
    xo_ref[...] = _layer_norm(ALPHA * xl_ref[...] + g2_ref[0] * f, lg_ref[...], lb_ref[...])


def _combine(dest, w, h, xl, g2, s1, s3, s2, ln_g, ln_b, ys, n_tok):
    tm = CB_TILE
    nt = n_tok // tm
    tok = lambda n: pl.BlockSpec((tm, n), lambda t: (t, 0))
    const = lambda r, n: pl.BlockSpec((r, n), lambda t: (0, 0))
    return pl.pallas_call(
        _combine_kernel,
        grid=(nt,),
        in_specs=[
            pl.BlockSpec((1, TOP_K, tm), lambda t: (t, 0, 0), memory_space=pltpu.SMEM),
            tok(TOP_K), tok(D_MODEL), tok(D_MODEL),
            pl.BlockSpec((1, 1, D_MODEL), lambda t: (_mod_row(t, SEQ // tm, N_LAT // tm), 0, 0)),
            const(D_MODEL, D_SHARED), const(D_MODEL, D_SHARED), const(D_SHARED, D_MODEL),
            const(1, D_MODEL), const(1, D_MODEL),
            pl.BlockSpec(memory_space=pl.ANY),
        ],
        out_specs=tok(D_MODEL),
        out_shape=jax.ShapeDtypeStruct((n_tok, D_MODEL), F32),
        scratch_shapes=[pltpu.VMEM((TOP_K, tm * TOK_ROWS, LANE), F32), pltpu.SemaphoreType.DMA],
        compiler_params=_cparams(("arbitrary",)),
        name="moe_combine",
    )(dest, w, h, xl, g2, s1, s3, s2, ln_g, ln_b, ys)


def _moe_layer(h, xl, g2, router_w_t, router_b, w1, w3, w2, s1, s3, s2, ln_g, ln_b, n_tok):
    top_e, rank, wts, cnt = _route(h, router_w_t, router_b, n_tok)
    counts = cnt[:, 0].astype(I32)
    padded = (counts + MOE_BLOCK - 1) // MOE_BLOCK * MOE_BLOCK
    pend = jnp.cumsum(padded)
    pstart = pend - padded
    n_blocks = n_tok * TOP_K // MOE_BLOCK + N_EXPERTS
    onehot = top_e[..., None] == jnp.arange(N_EXPERTS, dtype=I32)
    dest = rank + jnp.sum(jnp.where(onehot, pstart, 0), axis=-1)
    blk_expert = jnp.minimum(
        jnp.sum((pend // MOE_BLOCK)[None, :] <= jnp.arange(n_blocks, dtype=I32)[:, None], axis=1), N_EXPERTS - 1).astype(I32)
    n_active = (pend[-1:] // MOE_BLOCK).astype(I32)
    h_tiles = h.reshape(h.shape[0] * TOK_ROWS, LANE)
    xs = _dispatch(counts, pstart.astype(I32), n_active, dest, h_tiles, n_blocks * MOE_BLOCK)
    ys = _experts(blk_expert, n_active, xs, w1, w3, w2)
    nt = n_tok // CB_TILE
    regroup = lambda a: a.transpose(1, 0, 2).reshape(TOP_K, n_tok)
    dest_c = regroup(dest).reshape(TOP_K, nt, CB_TILE).transpose(1, 0, 2)
    w_c = regroup(wts).T
    return _combine(dest_c, w_c, h, xl, g2, s1, s3, s2, ln_g, ln_b, ys, n_tok)


def kernel(x, c, ctx, c_ctx, w_ada, b_ada, w_in, hy_conv_w, hy_conv_b, hy_w1, hy_b1, hy_w2, hy_b2, hy_w3, hy_sin_freq, hy_bias_d, hy_proj, sc_conv_w, sc_proj, na_rpb, na_proj, w_o, ln1_g, ln1_b, ln2_g, ln2_b, moe_router, moe_bias, moe_w1, moe_w3, moe_w2, sh_w1, sh_w3, sh_w2):
    bf = lambda a: a.astype(BF16)
    xa = jnp.concatenate([x.reshape(N_LAT, D_MODEL), ctx.reshape(N_CTX, D_MODEL)], axis=0)

    cc = jnp.zeros((16, D_MODEL), F32).at[:BATCH].set(c).at[BATCH].set(c_ctx)
    mods = _ada(cc, w_ada, b_ada)[:, :BATCH + 1].reshape(DEPTH, BATCH + 1, 6, 1, D_MODEL)
    filt_lat = _hyena_filters(SEQ, hy_w1, hy_b1, hy_w2, hy_b2, hy_w3, hy_sin_freq)
    filt_ctx = _hyena_filters(CTX_LEN, hy_w1, hy_b1, hy_w2, hy_b2, hy_w3, hy_sin_freq)

    offs = [0, 3 * D_HYENA, 3 * D_HYENA + 3 * D_SCONV]
    offs += [offs[2] + D_NA, offs[2] + 2 * D_NA, offs[2] + 3 * D_NA, w_in.shape[2]]

    for i in range(DEPTH):
        last = i == DEPTH - 1
        m = [mods[i, :, j] for j in range(6)]
        w_secs = [bf(w_in[i][:, offs[j]:offs[j + 1]]) for j in range(6)]
        u_hy, u_sc, q, k, v, gates = _proj(xa, m[0], m[1], w_secs, N_TOK)

        z, x0 = _hy_pre(u_hy, hy_conv_w[i], hy_conv_b[i][None], None, SEQ, 0)
        z, x0 = _hy_pre(u_hy, hy_conv_w[i], hy_conv_b[i][None], (z, x0), CTX_LEN, N_LAT)
        yc = jnp.concatenate([_hyena_conv(z, filt_lat[i], SEQ, 0), _hyena_conv(z, filt_ctx[i], CTX_LEN, N_LAT)], axis=0)
        ysc = _short_conv(u_sc, sc_conv_w[i], None, SEQ, 0)
        ysc = _short_conv(u_sc, sc_conv_w[i], ysc, CTX_LEN, N_LAT)
        att = _na_attention(q, k, v, _na_bias(na_rpb[i]))
        att = _ctx_attention(q, k, v, att)

        xa, hmoe = _mix(yc, z, x0, ysc, att, gates, xa, m[2], m[3], m[4], hy_bias_d[i][None],
                        bf(hy_proj[i]), bf(sc_proj[i]), bf(na_proj[i]), bf(w_o[i]), ln1_g[i][None], ln1_b[i][None], N_TOK)

        n_moe = N_LAT if last else N_TOK
        xa = _moe_layer(hmoe, xa, m[5], moe_router[i].T, moe_bias[i][:, None], bf(moe_w1[i]), bf(moe_w3[i]),
                        bf(moe_w2[i]), bf(sh_w1[i]), bf(sh_w3[i]), bf(sh_w2[i]), ln2_g[i][None], ln2_b[i][None], n_moe)
    return xa.reshape(BATCH, SEQ, D_MODEL)
```

```python
import functools
import math

import numpy as np
import jax
import jax.numpy as jnp
from jax import lax
from jax.experimental import pallas as pl
from jax.experimental.pallas import tpu as pltpu

F32 = jnp.float32
BF16 = jnp.bfloat16
I32 = jnp.int32
HIGHEST = lax.Precision.HIGHEST

D_MODEL = 1024
BATCH = 8
SEQ = 4096
DEPTH = 4
CTX_LEN = 256
GRID_W = 64
D_HYENA = 512
D_SCONV = 512
NA_HEADS = 8
NA_HEAD_DIM = 64
D_NA = NA_HEADS * NA_HEAD_DIM
NA_WIN_ROWS = 8
NA_WIN_COLS = 16
HY_BANDS = 16
HY_EMB = 1 + 2 * HY_BANDS
HY_FILTER_DIM = 64
HY_FAST_DECAY = 0.3
HY_SLOW_DECAY = 1.5
HY_TARGET = 1e-2
N_EXPERTS = 64
N_GROUPS = 8
TOPK_GROUPS = 4
TOP_K = 8
D_EXPERT = 256
D_SHARED = 256
ROUTED_SCALE = 2.5
LN_EPS = 1e-5
NEG_INF = -1e30
ALPHA = (2.0 * DEPTH) ** 0.25

N_LAT = BATCH * SEQ
N_CTX = BATCH * CTX_LEN
N_TOK = N_LAT + N_CTX

LANE = 128
SUBLANE = 8
VMEM_LIMIT = 56 * 1024 * 1024

TOK_TILE = 256
NA_QROWS = 4
NA_KROWS = NA_QROWS + NA_WIN_ROWS - 1
MOE_BLOCK = 512
RT_TILE = 512
CB_TILE = 128
LC_CH = 8
ROW_UNROLL = 4
WAIT_UNROLL = 16
TOK_ROWS = D_MODEL // LANE
assert TOK_ROWS == SUBLANE


def _cparams(sem):
    return pltpu.CompilerParams(dimension_semantics=sem, vmem_limit_bytes=VMEM_LIMIT)


def _mod_row(tile, tiles_per_batch, n_lat_tiles):
    return jnp.where(tile < n_lat_tiles, tile // tiles_per_batch, BATCH)


def _layer_norm(v, g, b):
    mu = jnp.mean(v, axis=-1, keepdims=True)
    c = v - mu
    var = jnp.mean(c * c, axis=-1, keepdims=True)
    return c * lax.rsqrt(var + LN_EPS) * g + b


def _ada_kernel(c_ref, w_ref, b_ref, o_ref):
    c = c_ref[...]
    cond = c * jax.nn.sigmoid(c)
    o_ref[0] = jnp.dot(cond, w_ref[0], precision=HIGHEST, preferred_element_type=F32) + b_ref[0]


def _ada(cc, w_ada, b_ada):
    depth, d, n = w_ada.shape
    tn = 1536
    return pl.pallas_call(
        _ada_kernel,
        grid=(depth, n // tn),
        in_specs=[
            pl.BlockSpec((16, d), lambda l, j: (0, 0)),
            pl.BlockSpec((1, d, tn), lambda l, j: (l, 0, j)),
            pl.BlockSpec((1, 1, tn), lambda l, j: (l, 0, j)),
        ],
        out_specs=pl.BlockSpec((1, 16, tn), lambda l, j: (l, 0, j)),
        out_shape=jax.ShapeDtypeStruct((depth, 16, n), F32),
        compiler_params=_cparams(("arbitrary", "arbitrary")),
        name="ada",
    )(cc, w_ada, b_ada.reshape(depth, 1, n))


def _proj_kernel(x_ref, sh_ref, sc_ref, w_hy, w_sc, w_q, w_k, w_v, w_g, o_hy, o_sc, o_q, o_k, o_v, o_g):
    h = (x_ref[...] * (1.0 + sc_ref[0]) + sh_ref[0]).astype(BF16)
    for w, o in ((w_hy, o_hy), (w_sc, o_sc), (w_q, o_q), (w_k, o_k), (w_v, o_v), (w_g, o_g)):
        o[...] = jnp.dot(h, w[...], preferred_element_type=F32).astype(o.dtype)


def _proj(x, shift, scale, ws, n_tok):
    tm = TOK_TILE
    nt = n_tok // tm
    mod = lambda t: (_mod_row(t, SEQ // tm, N_LAT // tm), 0, 0)
    widths = [w.shape[1] for w in ws]
    dtypes = [F32, F32, BF16, BF16, BF16, F32]
    return pl.pallas_call(
        _proj_kernel,
        grid=(nt,),
        in_specs=[
            pl.BlockSpec((tm, D_MODEL), lambda t: (t, 0)),
            pl.BlockSpec((1, 1, D_MODEL), mod),
            pl.BlockSpec((1, 1, D_MODEL), mod),
        ] + [pl.BlockSpec((D_MODEL, n), lambda t: (0, 0), pipeline_mode=pl.Buffered(1)) for n in widths],
        out_specs=[pl.BlockSpec((tm, n), lambda t: (t, 0)) for n in widths],
        out_shape=[jax.ShapeDtypeStruct((x.shape[0], n), dt) for n, dt in zip(widths, dtypes)],
        compiler_params=_cparams(("arbitrary",)),
        name="in_proj",
    )(x, shift, scale, *ws)


def _dwconv3(u, w):
    s = u.shape[0]
    row = lax.broadcasted_iota(I32, u.shape, 0)
    prev = jnp.where(row == 0, 0.0, pltpu.roll(u, 1, 0))
    nxt = jnp.where(row == s - 1, 0.0, pltpu.roll(u, s - 1, 0))
    return prev * w[0:1] + u * w[1:2] + nxt * w[2:3]


def _hy_pre_kernel(u0_ref, u1_ref, u2_ref, w0_ref, w1_ref, w2_ref, b0_ref, b1_ref, b2_ref, pz_ref, px_ref,
                   z_ref, x0_ref):
    del pz_ref, px_ref
    x0_ref[...] = _dwconv3(u0_ref[...], w0_ref[...]) + b0_ref[...]
    x1 = _dwconv3(u1_ref[...], w1_ref[...]) + b1_ref[...]
    v = _dwconv3(u2_ref[...], w2_ref[...]) + b2_ref[...]
    z_ref[...] = v * x1


def _hy_pre(u, conv_w, conv_b, prev, seq, row_off):
    ncb = D_HYENA // LANE
    ob = row_off // seq
    usp = lambda s: pl.BlockSpec((seq, LANE), lambda b, c: (ob + b, s * ncb + c))
    wsp = lambda s: pl.BlockSpec((3, LANE), lambda b, c: (0, s * ncb + c))
    bsp = lambda s: pl.BlockSpec((1, LANE), lambda b, c: (0, s * ncb + c))
    osp = pl.BlockSpec((seq, LANE), lambda b, c: (ob + b, c))
    n_tok = u.shape[0]
    if prev is None:
        prev = (jnp.zeros((n_tok, D_HYENA), F32), jnp.zeros((n_tok, D_HYENA), F32))
    args = [u, u, u, conv_w, conv_w, conv_w, conv_b, conv_b, conv_b] + list(prev)
    in_specs = [usp(0), usp(1), usp(2), wsp(0), wsp(1), wsp(2), bsp(0), bsp(1), bsp(2)]
    in_specs += [pl.BlockSpec(memory_space=pl.ANY)] * 2
    return pl.pallas_call(
        _hy_pre_kernel,
        grid=(BATCH, ncb),
        in_specs=in_specs,
        out_specs=[osp, osp],
        out_shape=[jax.ShapeDtypeStruct((n_tok, D_HYENA), F32)] * 2,
        input_output_aliases={9: 0, 10: 1},
        compiler_params=_cparams(("arbitrary", "arbitrary")),
        name="hyena_pre",
    )(*args)


def _sc_kernel(bg_ref, cg_ref, xs_ref, w_ref, prev_ref, o_ref):
    del prev_ref
    o_ref[...] = bg_ref[...] * _dwconv3(cg_ref[...] * xs_ref[...], w_ref[...])


def _short_conv(u, conv_w, prev, seq, row_off):
    ncb = D_SCONV // LANE
    ob = row_off // seq
    usp = lambda s: pl.BlockSpec((seq, LANE), lambda b, c: (ob + b, s * ncb + c))
    osp = pl.BlockSpec((seq, LANE), lambda b, c: (ob + b, c))
    if prev is None:
        prev = jnp.zeros((u.shape[0], D_SCONV), F32)
    in_specs = [usp(0), usp(1), usp(2), pl.BlockSpec((3, LANE), lambda b, c: (0, c)),
                pl.BlockSpec(memory_space=pl.ANY)]
    return pl.pallas_call(
        _sc_kernel,
        grid=(BATCH, ncb),
        in_specs=in_specs,
        out_specs=osp,
        out_shape=jax.ShapeDtypeStruct((u.shape[0], D_SCONV), F32),
        input_output_aliases={4: 0},
        compiler_params=_cparams(("arbitrary", "arbitrary")),
        name="short_conv",
    )(u, u, u, conv_w, prev)


def _filter_kernel(f_ref, dec_ref, w1_ref, b1_ref, w2_ref, b2_ref, w3_ref, fr_ref, o_ref):
    dot = functools.partial(jnp.dot, precision=HIGHEST, preferred_element_type=F32)
    z = jnp.sin(fr_ref[0, 0:1] * (dot(f_ref[...], w1_ref[0]) + b1_ref[0]))
    z = jnp.sin(fr_ref[0, 1:2] * (dot(z, w2_ref[0]) + b2_ref[0]))
    o_ref[0] = dot(z, w3_ref[0]) * dec_ref[...]


def _filter_consts(length):
    t = jnp.linspace(0.0, 1.0, length, dtype=F32)[:, None]
    ang = (2.0 * math.pi / length) * jnp.arange(length, dtype=F32)[:, None]
    bands = jnp.linspace(1e-4, HY_BANDS - 1, HY_BANDS, dtype=F32)[None, :]
    feats = jnp.concatenate([t, jnp.cos(bands * ang), -jnp.sin(bands * ang)], -1)
    feats = jnp.pad(feats, ((0, 0), (0, LANE - HY_EMB)))
    deltas = jnp.abs(jnp.linspace(math.log(HY_TARGET) / HY_SLOW_DECAY, math.log(HY_TARGET) / HY_FAST_DECAY,
                                  D_HYENA, dtype=F32))
    decay = jnp.exp(-t * deltas[None, :])
    return feats, jnp.concatenate([decay, decay], -1)


def _hyena_filters(length, w1, b1, w2, b2, w3, sin_freq):
    depth = w1.shape[0]
    pf = LANE - HY_FILTER_DIM
    w1p = jnp.pad(w1, ((0, 0), (0, LANE - HY_EMB), (0, pf)))
    b1p = jnp.pad(b1, ((0, 0), (0, pf)))[:, None, :]
    w2p = jnp.pad(w2, ((0, 0), (0, pf), (0, pf)))
    b2p = jnp.pad(b2, ((0, 0), (0, pf)))[:, None, :]
    w3p = jnp.pad(w3, ((0, 0), (0, pf), (0, 0)))
    frp = jnp.pad(sin_freq, ((0, 0), (0, 0), (0, pf)))
    feats, decay = _filter_consts(length)
    tl = min(length, 512)
    lsp = lambda shape: pl.BlockSpec((1,) + shape, lambda l, j: (l, 0, 0))
    h = pl.pallas_call(
        _filter_kernel,
        grid=(depth, length // tl),
        in_specs=[
            pl.BlockSpec((tl, LANE), lambda l, j: (j, 0)),
            pl.BlockSpec((tl, 2 * D_HYENA), lambda l, j: (j, 0)),
            lsp((LANE, LANE)), lsp((1, LANE)), lsp((LANE, LANE)), lsp((1, LANE)),
            lsp((LANE, 2 * D_HYENA)), lsp((2, LANE)),
        ],
        out_specs=pl.BlockSpec((1, tl, 2 * D_HYENA), lambda l, j: (l, j, 0)),
        out_shape=jax.ShapeDtypeStruct((depth, length, 2 * D_HYENA), F32),
        compiler_params=_cparams(("arbitrary", "arbitrary")),
        name="hyena_filter",
    )(feats, decay, w1p, b1p, w2p, b2p, w3p, frp)
    h_fwd = h[:, :, :D_HYENA]
    h_bwd = h[:, :, D_HYENA:]
    g_lin = jnp.concatenate([jnp.zeros((depth, 1, D_HYENA), F32), h_bwd[:, :0:-1], h_fwd], axis=1)
    return g_lin.transpose(0, 2, 1).reshape(depth, D_HYENA, 2 * length // LANE, LANE)


def _lconv_kernel(nb, z_ref, g_ref, y_ref, zl_ref):
    krow = lax.broadcasted_iota(I32, (LANE, LANE), 0)
    acol = lax.broadcasted_iota(I32, (LANE, LANE), 1)
    upper = acol >= krow

    def shifted(c, seg):
        return pltpu.roll(jnp.broadcast_to(g_ref[c, seg:seg + 1, :], (LANE, LANE)), 0, 1, stride=1, stride_axis=0)

    def per_channel(c, carry):
        for j in range(nb):
            zl_ref[j * BATCH:(j + 1) * BATCH, :] = z_ref[c, :, j * LANE:(j + 1) * LANE]
        y_ref[c] = jnp.zeros((nb * BATCH, LANE), F32)
        prev = shifted(c, 0)
        for d in range(1 - nb, nb):
            cur = shifted(c, d + nb)
            toep = jnp.where(upper, cur, prev).astype(BF16)
            j0, j1 = max(0, -d), min(nb, nb - d)
            part = jnp.dot(zl_ref[j0 * BATCH:j1 * BATCH, :].astype(BF16), toep, preferred_element_type=F32)
            y_ref[c, (j0 + d) * BATCH:(j1 + d) * BATCH, :] += part
            prev = cur
        return carry

    lax.fori_loop(0, LC_CH, per_channel, 0)


def _long_conv(zt, g):
    ch, _, length = zt.shape
    nb = length // LANE
    return pl.pallas_call(
        functools.partial(_lconv_kernel, nb),
        grid=(ch // LC_CH,),
        in_specs=[
            pl.BlockSpec((LC_CH, BATCH, length), lambda i: (i, 0, 0)),
            pl.BlockSpec((LC_CH, 2 * nb, LANE), lambda i: (i, 0, 0)),
        ],
        out_specs=pl.BlockSpec((LC_CH, nb * BATCH, LANE), lambda i: (i, 0, 0)),
        out_shape=jax.ShapeDtypeStruct((ch, nb * BATCH, LANE), F32),
        scratch_shapes=[pltpu.VMEM((nb * BATCH, LANE), F32)],
        compiler_params=_cparams(("arbitrary",)),
        name="hyena_long_conv",
    )(zt, g)


def _hyena_conv(z, g, seq, row_off):
    zt = z[row_off:row_off + BATCH * seq].reshape(BATCH, seq, D_HYENA).transpose(2, 0, 1)
    yt = _long_conv(zt, g)
    nb = seq // LANE
    return yt.reshape(D_HYENA, nb, BATCH, LANE).transpose(2, 1, 3, 0).reshape(BATCH * seq, D_HYENA)


def _na_bias_index():
    rows = SEQ // GRID_W
    ngrp = rows // NA_QROWS
    qcol = np.arange(GRID_W)
    cstart = np.clip(qcol - NA_WIN_COLS // 2, 0, GRID_W - NA_WIN_COLS)
    ridx = np.zeros((3, NA_QROWS, NA_KROWS), np.int32)
    valid = np.zeros((3, NA_QROWS, GRID_W, NA_KROWS, GRID_W), bool)
    for v, g in enumerate((0, 1, ngrp - 1)):
        u0 = int(np.clip(NA_QROWS * g - NA_WIN_ROWS // 2, 0, rows - NA_KROWS))
        for ri in range(NA_QROWS):
            r = NA_QROWS * g + ri
            rs = int(np.clip(r - NA_WIN_ROWS // 2, 0, rows - NA_WIN_ROWS))
            kr = u0 + np.arange(NA_KROWS)
            row_ok = (kr >= rs) & (kr < rs + NA_WIN_ROWS)
            col_ok = (qcol[None, :] >= cstart[:, None]) & (qcol[None, :] < cstart[:, None] + NA_WIN_COLS)
            ok = row_ok[None, :, None] & col_ok[:, None, :]
            valid[v, ri] = ok
            ridx[v, ri] = np.clip(kr - r + NA_WIN_ROWS - 1, 0, 2 * NA_WIN_ROWS - 2)
    rel = np.clip(qcol[None, :] - qcol[:, None] + NA_WIN_COLS - 1, 0, 2 * NA_WIN_COLS - 2)
    onehot = (rel.reshape(1, -1) == np.arange(2 * NA_WIN_COLS - 1)[:, None]).astype(np.float32)
    nq, nk = NA_QROWS * GRID_W, NA_KROWS * GRID_W
    return ridx, onehot, valid.reshape(3, nq, nk)


def _na_bias(rpb):
    ridx, onehot, valid = _na_bias_index()
    nrow = 2 * NA_WIN_ROWS - 1
    cols = jnp.dot(rpb.reshape(NA_HEADS * nrow, -1), jnp.asarray(onehot), precision=HIGHEST)
    cols = cols.reshape(NA_HEADS, nrow, GRID_W, GRID_W)
    b = cols[:, ridx]
    b = b.transpose(1, 0, 2, 4, 3, 5).reshape(3, NA_HEADS, NA_QROWS * GRID_W, NA_KROWS * GRID_W)
    return jnp.where(valid[:, None], b, NEG_INF)


def _softmax_av(s_list, v_list):
    m = functools.reduce(jnp.maximum, [s.max(axis=-1, keepdims=True) for s in s_list])
    ps = [jnp.exp(s - m) for s in s_list]
    den = functools.reduce(jnp.add, [p.sum(axis=-1, keepdims=True) for p in ps])
    o = functools.reduce(jnp.add, [jnp.dot(p.astype(BF16), v, preferred_element_type=F32) for p, v in zip(ps, v_list)])
    return o / den


def _qk(q, k):
    return lax.dot_general(q, k, (((1,), (1,)), ((), ())), preferred_element_type=F32)


def _na_kernel(q_ref, k_ref, v_ref, kc_ref, vc_ref, bias_ref, prev_ref, o_ref):
    del prev_ref
    rows = SEQ // GRID_W
    g = pl.program_id(1)
    u0 = jnp.clip(NA_QROWS * g - NA_WIN_ROWS // 2, 0, rows - NA_KROWS)
    start = pl.multiple_of(u0 * GRID_W, GRID_W)
    nk = NA_KROWS * GRID_W
    scale = NA_HEAD_DIM ** -0.5
    for h in range(NA_HEADS):
        sl = slice(h * NA_HEAD_DIM, (h + 1) * NA_HEAD_DIM)
        qh = q_ref[:, sl]
        kh = k_ref[pl.ds(start, nk), sl]
        vh = v_ref[pl.ds(start, nk), sl]
        s_loc = _qk(qh, kh) * scale + bias_ref[0, h]
        s_ctx = _qk(qh, kc_ref[:, sl]) * scale
        o_ref[:, sl] = _softmax_av([s_loc, s_ctx], [vh, vc_ref[:, sl]]).astype(o_ref.dtype)


def _na_attention(q, k, v, bias):
    nq = NA_QROWS * GRID_W
    ngrp = SEQ // nq
    ctx0 = N_LAT // CTX_LEN

    def variant(b, g):
        return (jnp.where(g == 0, 0, jnp.where(g == ngrp - 1, 2, 1)), 0, 0, 0)

    return pl.pallas_call(
        _na_kernel,
        grid=(BATCH, ngrp),
        in_specs=[
            pl.BlockSpec((nq, D_NA), lambda b, g: (b * ngrp + g, 0)),
            pl.BlockSpec((SEQ, D_NA), lambda b, g: (b, 0)),
            pl.BlockSpec((SEQ, D_NA), lambda b, g: (b, 0)),
            pl.BlockSpec((CTX_LEN, D_NA), lambda b, g: (ctx0 + b, 0)),
            pl.BlockSpec((CTX_LEN, D_NA), lambda b, g: (ctx0 + b, 0)),
            pl.BlockSpec((1, NA_HEADS, nq, NA_KROWS * GRID_W), variant),
            pl.BlockSpec(memory_space=pl.ANY),
        ],
        out_specs=pl.BlockSpec((nq, D_NA), lambda b, g: (b * ngrp + g, 0)),
        out_shape=jax.ShapeDtypeStruct((N_TOK, D_NA), BF16),
        input_output_aliases={6: 0},
        compiler_params=_cparams(("arbitrary", "arbitrary")),
        name="na_attention",
    )(q, k, v, k, v, bias, jnp.zeros((N_TOK, D_NA), BF16))


def _ctx_attn_kernel(q_ref, k_ref, v_ref, prev_ref, o_ref):
    del prev_ref
    scale = NA_HEAD_DIM ** -0.5
    for h in range(NA_HEADS):
        sl = slice(h * NA_HEAD_DIM, (h + 1) * NA_HEAD_DIM)
        s = _qk(q_ref[:, sl], k_ref[:, sl]) * scale
        o_ref[:, sl] = _softmax_av([s], [v_ref[:, sl]]).astype(o_ref.dtype)


def _ctx_attention(q, k, v, att):
    ctx0 = N_LAT // CTX_LEN
    sp = pl.BlockSpec((CTX_LEN, D_NA), lambda b: (ctx0 + b, 0))
    return pl.pallas_call(
        _ctx_attn_kernel,
        grid=(BATCH,),
        in_specs=[sp, sp, sp, pl.BlockSpec(memory_space=pl.ANY)],
        out_specs=sp,
        out_shape=jax.ShapeDtypeStruct(att.shape, att.dtype),
        input_output_aliases={3: 0},
        compiler_params=_cparams(("arbitrary",)),
        name="ctx_attention",
    )(q, k, v, att)


def _mix_kernel(yc_ref, z_ref, x0_ref, ysc_ref, ya_ref, gt_ref, xl_ref, g1_ref, sh2_ref, sc2_ref, bd_ref,
                wh_ref, ws_ref, wn_ref, wo_ref, lg_ref, lb_ref, xo_ref, ho_ref):
    dot = functools.partial(jnp.dot, preferred_element_type=F32)
    y_hy = x0_ref[...] * (yc_ref[...] + z_ref[...] * bd_ref[...])
    merged = (jax.nn.sigmoid(gt_ref[:, 0:D_MODEL]) * dot(y_hy.astype(BF16), wh_ref[...])
              + jax.nn.sigmoid(gt_ref[:, D_MODEL:2 * D_MODEL]) * dot(ysc_ref[...].astype(BF16), ws_ref[...])
              + jax.nn.sigmoid(gt_ref[:, 2 * D_MODEL:3 * D_MODEL]) * dot(ya_ref[...], wn_ref[...]))
    out = dot(merged.astype(BF16), wo_ref[...])
    xo = _layer_norm(ALPHA * xl_ref[...] + g1_ref[0] * out, lg_ref[...], lb_ref[...])
    xo_ref[...] = xo
    ho_ref[...] = xo * (1.0 + sc2_ref[0]) + sh2_ref[0]


def _mix(yc, z, x0, ysc, ya, gates, xl, g1, sh2, sc2, bias_d, wh, ws, wn, wo, ln_g, ln_b, n_tok):
    tm = TOK_TILE
    tok = lambda n: pl.BlockSpec((tm, n), lambda t: (t, 0))
    mod = pl.BlockSpec((1, 1, D_MODEL), lambda t: (_mod_row(t, SEQ // tm, N_LAT // tm), 0, 0))
    const = lambda r, n: pl.BlockSpec((r, n), lambda t: (0, 0))
    return pl.pallas_call(
        _mix_kernel,
        grid=(n_tok // tm,),
        in_specs=[tok(D_HYENA), tok(D_HYENA), tok(D_HYENA), tok(D_SCONV), tok(D_NA), tok(3 * D_MODEL), tok(D_MODEL),
                  mod, mod, mod, const(1, D_HYENA),
                  const(D_HYENA, D_MODEL), const(D_SCONV, D_MODEL), const(D_NA, D_MODEL), const(D_MODEL, D_MODEL),
                  const(1, D_MODEL), const(1, D_MODEL)],
        out_specs=[tok(D_MODEL), tok(D_MODEL)],
        out_shape=[jax.ShapeDtypeStruct((xl.shape[0], D_MODEL), F32)] * 2,
        compiler_params=_cparams(("arbitrary",)),
        name="mixer_out",
    )(yc, z, x0, ysc, ya, gates, xl, g1, sh2, sc2, bias_d, wh, ws, wn, wo, ln_g, ln_b)


def _router_kernel(h_ref, wr_ref, rb_ref, tri_ref, te_ref, rk_ref, w_ref, cnt_ref, run_ref):
    tm = h_ref.shape[0]
    per = N_EXPERTS // N_GROUPS

    @pl.when(pl.program_id(0) == 0)
    def _():
        run_ref[...] = jnp.zeros_like(run_ref)

    logits = lax.dot_general(wr_ref[...], h_ref[...], (((1,), (1,)), ((), ())),
                             precision=HIGHEST, preferred_element_type=F32)
    scores = jax.nn.sigmoid(logits)
    sel = scores + rb_ref[...]
    sub = lax.broadcasted_iota(I32, (per, tm), 0)
    colmax = lambda a: jnp.max(a, axis=0, keepdims=True)
    colmin = lambda a: jnp.min(a, axis=0, keepdims=True)
    ninf = -jnp.inf

    xs = [sel[g * per:(g + 1) * per, :] for g in range(N_GROUPS)]
    sc = [scores[g * per:(g + 1) * per, :] for g in range(N_GROUPS)]
    gs = []
    for x in xs:
        m1 = colmax(x)
        i1 = colmin(jnp.where(x == m1, sub, per))
        m2 = colmax(jnp.where(sub == i1, ninf, x))
        gs.append(m1 + m2)
    chosen = [jnp.zeros((1, tm), F32) for _ in range(N_GROUPS)]
    for _ in range(TOPK_GROUPS):
        gm = functools.reduce(jnp.maximum, gs)
        gi = jnp.full((1, tm), N_GROUPS, I32)
        for g in reversed(range(N_GROUPS)):
            gi = jnp.where(gs[g] == gm, g, gi)
        for g in range(N_GROUPS):
            hit = gi == g
            chosen[g] = jnp.where(hit, 1.0, chosen[g])
            gs[g] = jnp.where(hit, ninf, gs[g])
    xm = [jnp.where(jnp.broadcast_to(chosen[g], (per, tm)) > 0.5, xs[g], ninf) for g in range(N_GROUPS)]
    eidx = [sub + g * per for g in range(N_GROUPS)]
    picked = [jnp.zeros((per, tm), F32) for _ in range(N_GROUPS)]
    top_e, top_s = [], []
    for _ in range(TOP_K):
        em = functools.reduce(jnp.maximum, [colmax(x) for x in xm])
        ei = functools.reduce(jnp.minimum, [colmin(jnp.where(xm[g] == em, eidx[g], N_EXPERTS)) for g in range(N_GROUPS)])
        s_acc = jnp.zeros((1, tm), F32)
        for g in range(N_GROUPS):
            hit = eidx[g] == ei
            picked[g] = jnp.where(hit, 1.0, picked[g])
            xm[g] = jnp.where(hit, ninf, xm[g])
            s_acc = s_acc + jnp.sum(jnp.where(hit, sc[g], 0.0), axis=0, keepdims=True)
        top_e.append(ei)
        top_s.append(s_acc)
    den = functools.reduce(jnp.add, top_s)
    mask = jnp.concatenate(picked, axis=0)
    prefix = jnp.dot(mask.astype(BF16), tri_ref[...], preferred_element_type=F32)
    pos = run_ref[:, 0:1] + prefix
    posg = [pos[g * per:(g + 1) * per, :] for g in range(N_GROUPS)]
    for i in range(TOP_K):
        r_acc = jnp.zeros((1, tm), F32)
        for g in range(N_GROUPS):
            r_acc = r_acc + jnp.sum(jnp.where(eidx[g] == top_e[i], posg[g], 0.0), axis=0, keepdims=True)
        te_ref[0, i:i + 1, :] = top_e[i]
        rk_ref[0, i:i + 1, :] = r_acc.astype(I32)
        w_ref[0, i:i + 1, :] = top_s[i] / den * ROUTED_SCALE
    run_ref[...] = run_ref[...] + jnp.sum(mask, axis=1, keepdims=True)
    cnt_ref[...] = run_ref[...]


def _route(tokens, router_w_t, router_b, n_tok):
    tm = RT_TILE
    nt = n_tok // tm
    tri = jnp.asarray(np.triu(np.ones((tm, tm), np.float32), 1), BF16)
    out3 = lambda dt: jax.ShapeDtypeStruct((nt, TOP_K, tm), dt)
    osp = pl.BlockSpec((1, TOP_K, tm), lambda t: (t, 0, 0))
    return pl.pallas_call(
        _router_kernel,
        grid=(nt,),
        in_specs=[
            pl.BlockSpec((tm, D_MODEL), lambda t: (t, 0)),
            pl.BlockSpec((N_EXPERTS, D_MODEL), lambda t: (0, 0)),
            pl.BlockSpec((N_EXPERTS, 1), lambda t: (0, 0)),
            pl.BlockSpec((tm, tm), lambda t: (0, 0)),
        ],
        out_specs=[osp, osp, osp, pl.BlockSpec((N_EXPERTS, LANE), lambda t: (0, 0))],
        out_shape=[out3(I32), out3(I32), out3(F32), jax.ShapeDtypeStruct((N_EXPERTS, LANE), F32)],
        scratch_shapes=[pltpu.VMEM((N_EXPERTS, LANE), F32)],
        compiler_params=_cparams(("arbitrary",)),
        name="moe_router",
    )(tokens, router_w_t, router_b, tri)


def _tok_rows(t):
    return pl.ds(pl.multiple_of(t * TOK_ROWS, TOK_ROWS), TOK_ROWS)


def _dispatch_kernel(cnt_ref, pst_ref, na_ref, dest_ref, h_ref, xs_hbm, zero_ref, sem):
    tm = dest_ref.shape[2]
    row_copy = lambda src, dst: pltpu.make_async_copy(src, xs_hbm.at[_tok_rows(dst)], sem)
    blk_rows = MOE_BLOCK * TOK_ROWS
    n_blocks = xs_hbm.shape[0] // blk_rows

    @pl.when(pl.program_id(0) == 0)
    def _():
        zero_ref[...] = jnp.zeros_like(zero_ref)

        def blk_copy(blk):
            return pltpu.make_async_copy(zero_ref, xs_hbm.at[pl.ds(pl.multiple_of(blk * blk_rows, blk_rows), blk_rows)], sem)

        def fill_blk(blk, c):
            blk_copy(blk).start()
            return c

        def drain_blk(blk, c):
            blk_copy(blk).wait()
            return c

        lax.fori_loop(na_ref[0], n_blocks, fill_blk, 0)
        lax.fori_loop(na_ref[0], n_blocks, drain_blk, 0)

        def per_expert(e, carry):
            n = cnt_ref[e]
            n_pad = (n + MOE_BLOCK - 1) // MOE_BLOCK * MOE_BLOCK - n
            base = pst_ref[e] + n

            def fill(r, c):
                row_copy(zero_ref.at[_tok_rows(0)], base + r).start()
                return c

            def drain(r, c):
                row_copy(zero_ref.at[_tok_rows(0)], base).wait()
                return c

            lax.fori_loop(0, n_pad, fill, 0)
            lax.fori_loop(0, n_pad, drain, 0)
            return carry

        lax.fori_loop(0, N_EXPERTS, per_expert, 0)

    def issue(tb, carry):
        for tt in range(ROW_UNROLL):
            t = tb * ROW_UNROLL + tt
            for i in range(TOP_K):
                row_copy(h_ref.at[_tok_rows(t)], dest_ref[0, i, t]).start(priority=i % 2)
        return carry

    def drain(tb, carry):
        for _ in range(WAIT_UNROLL * TOP_K):
            row_copy(h_ref.at[_tok_rows(0)], 0).wait()
        return carry

    lax.fori_loop(0, tm // ROW_UNROLL, issue, 0)
    lax.fori_loop(0, tm // WAIT_UNROLL, drain, 0)


def _dispatch(counts, pstart, n_active, dest, tokens, n_slots):
    nt, _, tm = dest.shape
    grid_spec = pltpu.PrefetchScalarGridSpec(
        num_scalar_prefetch=3,
        grid=(nt,),
        in_specs=[
            pl.BlockSpec((1, TOP_K, tm), lambda t, c, p, a: (t, 0, 0), memory_space=pltpu.SMEM),
            pl.BlockSpec((tm * TOK_ROWS, LANE), lambda t, c, p, a: (t, 0)),
        ],
        out_specs=pl.BlockSpec(memory_space=pl.ANY),
        scratch_shapes=[pltpu.VMEM((MOE_BLOCK * TOK_ROWS, LANE), F32), pltpu.SemaphoreType.DMA],
    )
    return pl.pallas_call(
        _dispatch_kernel,
        grid_spec=grid_spec,
        out_shape=jax.ShapeDtypeStruct((n_slots * TOK_ROWS, LANE), F32),
        compiler_params=_cparams(("arbitrary",)),
        name="moe_dispatch",
    )(counts, pstart, n_active, dest, tokens)


def _expert_kernel(be_ref, na_ref, x_ref, w1_ref, w3_ref, w2_ref, y_ref):
    active = pl.program_id(0) < na_ref[0]

    @pl.when(active)
    def _():
        dot = functools.partial(jnp.dot, preferred_element_type=F32)
        x = jnp.concatenate([x_ref[pl.ds(j, MOE_BLOCK, stride=TOK_ROWS), :] for j in range(TOK_ROWS)], axis=-1)
        x = x.astype(BF16)
        a = dot(x, w1_ref[0])
        hidden = (a * jax.nn.sigmoid(a)) * dot(x, w3_ref[0])
        y = dot(hidden.astype(BF16), w2_ref[0])
        for j in range(TOK_ROWS):
            y_ref[pl.ds(j, MOE_BLOCK, stride=TOK_ROWS), :] = y[:, j * LANE:(j + 1) * LANE]

    @pl.when(jnp.logical_not(active))
    def _():
        y_ref[...] = jnp.zeros_like(y_ref)


def _experts(blk_expert, n_active, xs, w1, w3, w2):
    blk_rows = MOE_BLOCK * TOK_ROWS
    n_blocks = xs.shape[0] // blk_rows
    xmap = lambda i, be, na: (jnp.minimum(i, na[0] - 1), 0)
    wmap = lambda i, be, na: (be[i], 0, 0)
    grid_spec = pltpu.PrefetchScalarGridSpec(
        num_scalar_prefetch=2,
        grid=(n_blocks,),
        in_specs=[
            pl.BlockSpec((blk_rows, LANE), xmap),
            pl.BlockSpec((1, D_MODEL, D_EXPERT), wmap),
            pl.BlockSpec((1, D_MODEL, D_EXPERT), wmap),
            pl.BlockSpec((1, D_EXPERT, D_MODEL), wmap),
        ],
        out_specs=pl.BlockSpec((blk_rows, LANE), lambda i, be, na: (i, 0)),
    )
    return pl.pallas_call(
        _expert_kernel,
        grid_spec=grid_spec,
        out_shape=jax.ShapeDtypeStruct(xs.shape, F32),
        compiler_params=_cparams(("arbitrary",)),
        name="moe_experts",
    )(blk_expert, n_active, xs, w1, w3, w2)


def _combine_kernel(dest_ref, w_ref, h_ref, xl_ref, g2_ref, s1_ref, s3_ref, s2_ref, lg_ref, lb_ref, ys_hbm,
                    xo_ref, ybuf, sem):
    tm = h_ref.shape[0]

    def row_copy(i, t, src):
        return pltpu.make_async_copy(ys_hbm.at[_tok_rows(src)], ybuf.at[i, _tok_rows(t)], sem)

    def issue(tb, carry):
        for tt in range(ROW_UNROLL):
            t = tb * ROW_UNROLL + tt
            for i in range(TOP_K):
                row_copy(i, t, dest_ref[0, i, t]).start(priority=i % 2)
        return carry

    def drain(tb, carry):
        for _ in range(WAIT_UNROLL * TOP_K):
            row_copy(0, 0, 0).wait()
        return carry

    lax.fori_loop(0, tm // ROW_UNROLL, issue, 0)
    dot = functools.partial(jnp.dot, preferred_element_type=F32)
    hb = h_ref[...].astype(BF16)
    a = dot(hb, s1_ref[...])
    f = dot(((a * jax.nn.sigmoid(a)) * dot(hb, s3_ref[...])).astype(BF16), s2_ref[...])
    lax.fori_loop(0, tm // WAIT_UNROLL, drain, 0)
    routed = []
    for j in range(TOK_ROWS):
        routed.append(functools.reduce(jnp.add, [w_ref[:, i:i + 1] * ybuf[i, pl.ds(j, tm, stride=TOK_ROWS), :]
                                                 for i in range(TOP_K)]))
    f = f + jnp.concatenate(routed, axis=-1)
    xo_ref[...] = _layer_norm(ALPHA * xl_ref[...] + g2_ref[0] * f, lg_ref[...], lb_ref[...])


def _combine(dest, w, h, xl, g2, s1, s3, s2, ln_g, ln_b, ys, n_tok):
    tm = CB_TILE
    nt = n_tok // tm
    tok = lambda n: pl.BlockSpec((tm, n), lambda t: (t, 0))
    const = lambda r, n: pl.BlockSpec((r, n), lambda t: (0, 0))
    return pl.pallas_call(
        _combine_kernel,
        grid=(nt,),
        in_specs=[
            pl.BlockSpec((1, TOP_K, tm), lambda t: (t, 0, 0), memory_space=pltpu.SMEM),
            tok(TOP_K), tok(D_MODEL), tok(D_MODEL),
            pl.BlockSpec((1, 1, D_MODEL), lambda t: (_mod_row(t, SEQ // tm, N_LAT // tm), 0, 0)),
            const(D_MODEL, D_SHARED), const(D_MODEL, D_SHARED), const(D_SHARED, D_MODEL),
            const(1, D_MODEL), const(1, D_MODEL),
            pl.BlockSpec(memory_space=pl.ANY),
        ],
        out_specs=tok(D_MODEL),
        out_shape=jax.ShapeDtypeStruct((n_tok, D_MODEL), F32),
        scratch_shapes=[pltpu.VMEM((TOP_K, tm * TOK_ROWS, LANE), F32), pltpu.SemaphoreType.DMA],
        compiler_params=_cparams(("arbitrary",)),
        name="moe_combine",
    )(dest, w, h, xl, g2, s1, s3, s2, ln_g, ln_b, ys)


def _moe_layer(h, xl, g2, router_w_t, router_b, w1, w3, w2, s1, s3, s2, ln_g, ln_b, n_tok):
    top_e, rank, wts, cnt = _route(h, router_w_t, router_b, n_tok)
    counts = cnt[:, 0].astype(I32)
    padded = (counts + MOE_BLOCK - 1) // MOE_BLOCK * MOE_BLOCK
    pend = jnp.cumsum(padded)
    pstart = pend - padded
    n_blocks = n_tok * TOP_K // MOE_BLOCK + N_EXPERTS
    onehot = top_e[..., None] == jnp.arange(N_EXPERTS, dtype=I32)
    dest = rank + jnp.sum(jnp.where(onehot, pstart, 0), axis=-1)
    blk_expert = jnp.minimum(
        jnp.sum((pend // MOE_BLOCK)[None, :] <= jnp.arange(n_blocks, dtype=I32)[:, None], axis=1), N_EXPERTS - 1).astype(I32)
    n_active = (pend[-1:] // MOE_BLOCK).astype(I32)
    h_tiles = h.reshape(h.shape[0] * TOK_ROWS, LANE)
    xs = _dispatch(counts, pstart.astype(I32), n_active, dest, h_tiles, n_blocks * MOE_BLOCK)
    ys = _experts(blk_expert, n_active, xs, w1, w3, w2)
    nt = n_tok // CB_TILE
    regroup = lambda a: a.transpose(1, 0, 2).reshape(TOP_K, n_tok)
    dest_c = regroup(dest).reshape(TOP_K, nt, CB_TILE).transpose(1, 0, 2)
    w_c = regroup(wts).T
    return _combine(dest_c, w_c, h, xl, g2, s1, s3, s2, ln_g, ln_b, ys, n_tok)


def kernel(x, c, ctx, c_ctx, w_ada, b_ada, w_in, hy_conv_w, hy_conv_b, hy_w1, hy_b1, hy_w2, hy_b2, hy_w3, hy_sin_freq, hy_bias_d, hy_proj, sc_conv_w, sc_proj, na_rpb, na_proj, w_o, ln1_g, ln1_b, ln2_g, ln2_b, moe_router, moe_bias, moe_w1, moe_w3, moe_w2, sh_w1, sh_w3, sh_w2):
    bf = lambda a: a.astype(BF16)
    xa = jnp.concatenate([x.reshape(N_LAT, D_MODEL), ctx.reshape(N_CTX, D_MODEL)], axis=0)

    cc = jnp.zeros((16, D_MODEL), F32).at[:BATCH].set(c).at[BATCH].set(c_ctx)
    mods = _ada(cc, w_ada, b_ada)[:, :BATCH + 1].reshape(DEPTH, BATCH + 1, 6, 1, D_MODEL)
    filt_lat = _hyena_filters(SEQ, hy_w1, hy_b1, hy_w2, hy_b2, hy_w3, hy_sin_freq)
    filt_ctx = _hyena_filters(CTX_LEN, hy_w1, hy_b1, hy_w2, hy_b2, hy_w3, hy_sin_freq)

    offs = [0, 3 * D_HYENA, 3 * D_HYENA + 3 * D_SCONV]
    offs += [offs[2] + D_NA, offs[2] + 2 * D_NA, offs[2] + 3 * D_NA, w_in.shape[2]]

    for i in range(DEPTH):
        last = i == DEPTH - 1
        m = [mods[i, :, j] for j in range(6)]
        w_secs = [bf(w_in[i][:, offs[j]:offs[j + 1]]) for j in range(6)]
        u_hy, u_sc, q, k, v, gates = _proj(xa, m[0], m[1], w_secs, N_TOK)

        z, x0 = _hy_pre(u_hy, hy_conv_w[i], hy_conv_b[i][None], None, SEQ, 0)
        z, x0 = _hy_pre(u_hy, hy_conv_w[i], hy_conv_b[i][None], (z, x0), CTX_LEN, N_LAT)
        yc = jnp.concatenate([_hyena_conv(z, filt_lat[i], SEQ, 0), _hyena_conv(z, filt_ctx[i], CTX_LEN, N_LAT)], axis=0)
        ysc = _short_conv(u_sc, sc_conv_w[i], None, SEQ, 0)
        ysc = _short_conv(u_sc, sc_conv_w[i], ysc, CTX_LEN, N_LAT)
        att = _na_attention(q, k, v, _na_bias(na_rpb[i]))
        att = _ctx_attention(q, k, v, att)

        xa, hmoe = _mix(yc, z, x0, ysc, att, gates, xa, m[2], m[3], m[4], hy_bias_d[i][None],
                        bf(hy_proj[i]), bf(sc_proj[i]), bf(na_proj[i]), bf(w_o[i]), ln1_g[i][None], ln1_b[i][None], N_TOK)

        n_moe = N_LAT if last else N_TOK
        xa = _moe_layer(hmoe, xa, m[5], moe_router[i].T, moe_bias[i][:, None], bf(moe_w1[i]), bf(moe_w3[i]),
                        bf(moe_w2[i]), bf(sh_w1[i]), bf(sh_w3[i]), bf(sh_w2[i]), ln2_g[i][None], ln2_b[i][None], n_moe)
    return xa.reshape(BATCH, SEQ, D_MODEL)
```

```python
import functools
import math

import numpy as np
import jax
import jax.numpy as jnp
from jax import lax
from jax.experimental import pallas as pl
from jax.experimental.pallas import tpu as pltpu

F32 = jnp.float32
BF16 = jnp.bfloat16
I32 = jnp.int32
HIGHEST = lax.Precision.HIGHEST

D_MODEL = 1024
BATCH = 8
SEQ = 4096
DEPTH = 4
CTX_LEN = 256
GRID_W = 64
D_HYENA = 512
D_SCONV = 512
NA_HEADS = 8
NA_HEAD_DIM = 64
D_NA = NA_HEADS * NA_HEAD_DIM
NA_WIN_ROWS = 8
NA_WIN_COLS = 16
HY_BANDS = 16
HY_EMB = 1 + 2 * HY_BANDS
HY_FILTER_DIM = 64
HY_FAST_DECAY = 0.3
HY_SLOW_DECAY = 1.5
HY_TARGET = 1e-2
N_EXPERTS = 64
N_GROUPS = 8
TOPK_GROUPS = 4
TOP_K = 8
D_EXPERT = 256
D_SHARED = 256
ROUTED_SCALE = 2.5
LN_EPS = 1e-5
NEG_INF = -1e30
ALPHA = (2.0 * DEPTH) ** 0.25

N_LAT = BATCH * SEQ
N_CTX = BATCH * CTX_LEN
N_TOK = N_LAT + N_CTX

LANE = 128
SUBLANE = 8
VMEM_LIMIT = 56 * 1024 * 1024

TOK_TILE = 256
NA_QROWS = 4
NA_KROWS = NA_QROWS + NA_WIN_ROWS - 1
MOE_BLOCK = 512
RT_TILE = 512
CB_TILE = 128
LC_CH = 8
ROW_UNROLL = 4
WAIT_UNROLL = 16
TOK_ROWS = 1
PACK_W = D_MODEL // 2
U32 = jnp.uint32


def _cparams(sem):
    return pltpu.CompilerParams(dimension_semantics=sem, vmem_limit_bytes=VMEM_LIMIT)


def _mod_row(tile, tiles_per_batch, n_lat_tiles):
    return jnp.where(tile < n_lat_tiles, tile // tiles_per_batch, BATCH)


def _layer_norm(v, g, b):
    mu = jnp.mean(v, axis=-1, keepdims=True)
    c = v - mu
    var = jnp.mean(c * c, axis=-1, keepdims=True)
    return c * lax.rsqrt(var + LN_EPS) * g + b


def _ada_kernel(c_ref, w_ref, b_ref, o_ref):
    c = c_ref[...]
    cond = c * jax.nn.sigmoid(c)
    o_ref[0] = jnp.dot(cond, w_ref[0], precision=HIGHEST, preferred_element_type=F32) + b_ref[0]


def _ada(cc, w_ada, b_ada):
    depth, d, n = w_ada.shape
    tn = 1536
    return pl.pallas_call(
        _ada_kernel,
        grid=(depth, n // tn),
        in_specs=[
            pl.BlockSpec((16, d), lambda l, j: (0, 0)),
            pl.BlockSpec((1, d, tn), lambda l, j: (l, 0, j)),
            pl.BlockSpec((1, 1, tn), lambda l, j: (l, 0, j)),
        ],
        out_specs=pl.BlockSpec((1, 16, tn), lambda l, j: (l, 0, j)),
        out_shape=jax.ShapeDtypeStruct((depth, 16, n), F32),
        compiler_params=_cparams(("arbitrary", "arbitrary")),
        name="ada",
    )(cc, w_ada, b_ada.reshape(depth, 1, n))


def _proj_kernel(x_ref, sh_ref, sc_ref, w_hy, w_sc, w_q, w_k, w_v, w_g, o_hy, o_sc, o_q, o_k, o_v, o_g):
    h = (x_ref[...] * (1.0 + sc_ref[0]) + sh_ref[0]).astype(BF16)
    for w, o in ((w_hy, o_hy), (w_sc, o_sc), (w_q, o_q), (w_k, o_k), (w_v, o_v), (w_g, o_g)):
        o[...] = jnp.dot(h, w[...], preferred_element_type=F32).astype(o.dtype)


def _proj(x, shift, scale, ws, n_tok):
    tm = TOK_TILE
    nt = n_tok // tm
    mod = lambda t: (_mod_row(t, SEQ // tm, N_LAT // tm), 0, 0)
    widths = [w.shape[1] for w in ws]
    dtypes = [F32, F32, BF16, BF16, BF16, F32]
    return pl.pallas_call(
        _proj_kernel,
        grid=(nt,),
        in_specs=[
            pl.BlockSpec((tm, D_MODEL), lambda t: (t, 0)),
            pl.BlockSpec((1, 1, D_MODEL), mod),
            pl.BlockSpec((1, 1, D_MODEL), mod),
        ] + [pl.BlockSpec((D_MODEL, n), lambda t: (0, 0), pipeline_mode=pl.Buffered(1)) for n in widths],
        out_specs=[pl.BlockSpec((tm, n), lambda t: (t, 0)) for n in widths],
        out_shape=[jax.ShapeDtypeStruct((x.shape[0], n), dt) for n, dt in zip(widths, dtypes)],
        compiler_params=_cparams(("arbitrary",)),
        name="in_proj",
    )(x, shift, scale, *ws)


def _dwconv3(u, w):
    s = u.shape[0]
    row = lax.broadcasted_iota(I32, u.shape, 0)
    prev = jnp.where(row == 0, 0.0, pltpu.roll(u, 1, 0))
    nxt = jnp.where(row == s - 1, 0.0, pltpu.roll(u, s - 1, 0))
    return prev * w[0:1] + u * w[1:2] + nxt * w[2:3]


def _hy_pre_kernel(u0_ref, u1_ref, u2_ref, w0_ref, w1_ref, w2_ref, b0_ref, b1_ref, b2_ref, pz_ref, px_ref,
                   z_ref, x0_ref):
    del pz_ref, px_ref
    x0_ref[...] = _dwconv3(u0_ref[...], w0_ref[...]) + b0_ref[...]
    x1 = _dwconv3(u1_ref[...], w1_ref[...]) + b1_ref[...]
    v = _dwconv3(u2_ref[...], w2_ref[...]) + b2_ref[...]
    z_ref[...] = v * x1


def _hy_pre(u, conv_w, conv_b, prev, seq, row_off):
    ncb = D_HYENA // LANE
    ob = row_off // seq
    usp = lambda s: pl.BlockSpec((seq, LANE), lambda b, c: (ob + b, s * ncb + c))
    wsp = lambda s: pl.BlockSpec((3, LANE), lambda b, c: (0, s * ncb + c))
    bsp = lambda s: pl.BlockSpec((1, LANE), lambda b, c: (0, s * ncb + c))
    osp = pl.BlockSpec((seq, LANE), lambda b, c: (ob + b, c))
    n_tok = u.shape[0]
    if prev is None:
        prev = (jnp.zeros((n_tok, D_HYENA), F32), jnp.zeros((n_tok, D_HYENA), F32))
    args = [u, u, u, conv_w, conv_w, conv_w, conv_b, conv_b, conv_b] + list(prev)
    in_specs = [usp(0), usp(1), usp(2), wsp(0), wsp(1), wsp(2), bsp(0), bsp(1), bsp(2)]
    in_specs += [pl.BlockSpec(memory_space=pl.ANY)] * 2
    return pl.pallas_call(
        _hy_pre_kernel,
        grid=(BATCH, ncb),
        in_specs=in_specs,
        out_specs=[osp, osp],
        out_shape=[jax.ShapeDtypeStruct((n_tok, D_HYENA), F32)] * 2,
        input_output_aliases={9: 0, 10: 1},
        compiler_params=_cparams(("arbitrary", "arbitrary")),
        name="hyena_pre",
    )(*args)


def _sc_kernel(bg_ref, cg_ref, xs_ref, w_ref, prev_ref, o_ref):
    del prev_ref
    o_ref[...] = bg_ref[...] * _dwconv3(cg_ref[...] * xs_ref[...], w_ref[...])


def _short_conv(u, conv_w, prev, seq, row_off):
    ncb = D_SCONV // LANE
    ob = row_off // seq
    usp = lambda s: pl.BlockSpec((seq, LANE), lambda b, c: (ob + b, s * ncb + c))
    osp = pl.BlockSpec((seq, LANE), lambda b, c: (ob + b, c))
    if prev is None:
        prev = jnp.zeros((u.shape[0], D_SCONV), F32)
    in_specs = [usp(0), usp(1), usp(2), pl.BlockSpec((3, LANE), lambda b, c: (0, c)),
                pl.BlockSpec(memory_space=pl.ANY)]
    return pl.pallas_call(
        _sc_kernel,
        grid=(BATCH, ncb),
        in_specs=in_specs,
        out_specs=osp,
        out_shape=jax.ShapeDtypeStruct((u.shape[0], D_SCONV), F32),
        input_output_aliases={4: 0},
        compiler_params=_cparams(("arbitrary", "arbitrary")),
        name="short_conv",
    )(u, u, u, conv_w, prev)


def _filter_kernel(f_ref, dec_ref, w1_ref, b1_ref, w2_ref, b2_ref, w3_ref, fr_ref, o_ref):
    dot = functools.partial(jnp.dot, precision=HIGHEST, preferred_element_type=F32)
    z = jnp.sin(fr_ref[0, 0:1] * (dot(f_ref[...], w1_ref[0]) + b1_ref[0]))
    z = jnp.sin(fr_ref[0, 1:2] * (dot(z, w2_ref[0]) + b2_ref[0]))
    o_ref[0] = dot(z, w3_ref[0]) * dec_ref[...]


def _filter_consts(length):
    t = jnp.linspace(0.0, 1.0, length, dtype=F32)[:, None]
    ang = (2.0 * math.pi / length) * jnp.arange(length, dtype=F32)[:, None]
    bands = jnp.linspace(1e-4, HY_BANDS - 1, HY_BANDS, dtype=F32)[None, :]
    feats = jnp.concatenate([t, jnp.cos(bands * ang), -jnp.sin(bands * ang)], -1)
    feats = jnp.pad(feats, ((0, 0), (0, LANE - HY_EMB)))
    deltas = jnp.abs(jnp.linspace(math.log(HY_TARGET) / HY_SLOW_DECAY, math.log(HY_TARGET) / HY_FAST_DECAY,
                                  D_HYENA, dtype=F32))
    decay = jnp.exp(-t * deltas[None, :])
    return feats, jnp.concatenate([decay, decay], -1)


def _hyena_filters(length, w1, b1, w2, b2, w3, sin_freq):
    depth = w1.shape[0]
    pf = LANE - HY_FILTER_DIM
    w1p = jnp.pad(w1, ((0, 0), (0, LANE - HY_EMB), (0, pf)))
    b1p = jnp.pad(b1, ((0, 0), (0, pf)))[:, None, :]
    w2p = jnp.pad(w2, ((0, 0), (0, pf), (0, pf)))
    b2p = jnp.pad(b2, ((0, 0), (0, pf)))[:, None, :]
    w3p = jnp.pad(w3, ((0, 0), (0, pf), (0, 0)))
    frp = jnp.pad(sin_freq, ((0, 0), (0, 0), (0, pf)))
    feats, decay = _filter_consts(length)
    tl = min(length, 512)
    lsp = lambda shape: pl.BlockSpec((1,) + shape, lambda l, j: (l, 0, 0))
    h = pl.pallas_call(
        _filter_kernel,
        grid=(depth, length // tl),
        in_specs=[
            pl.BlockSpec((tl, LANE), lambda l, j: (j, 0)),
            pl.BlockSpec((tl, 2 * D_HYENA), lambda l, j: (j, 0)),
            lsp((LANE, LANE)), lsp((1, LANE)), lsp((LANE, LANE)), lsp((1, LANE)),
            lsp((LANE, 2 * D_HYENA)), lsp((2, LANE)),
        ],
        out_specs=pl.BlockSpec((1, tl, 2 * D_HYENA), lambda l, j: (l, j, 0)),
        out_shape=jax.ShapeDtypeStruct((depth, length, 2 * D_HYENA), F32),
        compiler_params=_cparams(("arbitrary", "arbitrary")),
        name="hyena_filter",
    )(feats, decay, w1p, b1p, w2p, b2p, w3p, frp)
    h_fwd = h[:, :, :D_HYENA]
    h_bwd = h[:, :, D_HYENA:]
    g_lin = jnp.concatenate([jnp.zeros((depth, 1, D_HYENA), F32), h_bwd[:, :0:-1], h_fwd], axis=1)
    return g_lin.transpose(0, 2, 1).reshape(depth, D_HYENA, 2 * length // LANE, LANE)


def _lconv_kernel(nb, z_ref, g_ref, y_ref, zl_ref):
    krow = lax.broadcasted_iota(I32, (LANE, LANE), 0)
    acol = lax.broadcasted_iota(I32, (LANE, LANE), 1)
    upper = acol >= krow

    def shifted(c, seg):
        return pltpu.roll(jnp.broadcast_to(g_ref[c, seg:seg + 1, :], (LANE, LANE)), 0, 1, stride=1, stride_axis=0)

    def per_channel(c, carry):
        for j in range(nb):
            zl_ref[j * BATCH:(j + 1) * BATCH, :] = z_ref[c, :, j * LANE:(j + 1) * LANE]
        y_ref[c] = jnp.zeros((nb * BATCH, LANE), F32)
        prev = shifted(c, 0)
        for d in range(1 - nb, nb):
            cur = shifted(c, d + nb)
            toep = jnp.where(upper, cur, prev).astype(BF16)
            j0, j1 = max(0, -d), min(nb, nb - d)
            part = jnp.dot(zl_ref[j0 * BATCH:j1 * BATCH, :].astype(BF16), toep, preferred_element_type=F32)
            y_ref[c, (j0 + d) * BATCH:(j1 + d) * BATCH, :] += part
            prev = cur
        return carry

    lax.fori_loop(0, LC_CH, per_channel, 0)


def _long_conv(zt, g):
    ch, _, length = zt.shape
    nb = length // LANE
    return pl.pallas_call(
        functools.partial(_lconv_kernel, nb),
        grid=(ch // LC_CH,),
        in_specs=[
            pl.BlockSpec((LC_CH, BATCH, length), lambda i: (i, 0, 0)),
            pl.BlockSpec((LC_CH, 2 * nb, LANE), lambda i: (i, 0, 0)),
        ],
        out_specs=pl.BlockSpec((LC_CH, nb * BATCH, LANE), lambda i: (i, 0, 0)),
        out_shape=jax.ShapeDtypeStruct((ch, nb * BATCH, LANE), F32),
        scratch_shapes=[pltpu.VMEM((nb * BATCH, LANE), F32)],
        compiler_params=_cparams(("arbitrary",)),
        name="hyena_long_conv",
    )(zt, g)


def _hyena_conv(z, g, seq, row_off):
    zt = z[row_off:row_off + BATCH * seq].reshape(BATCH, seq, D_HYENA).transpose(2, 0, 1)
    yt = _long_conv(zt, g)
    nb = seq // LANE
    return yt.reshape(D_HYENA, nb, BATCH, LANE).transpose(2, 1, 3, 0).reshape(BATCH * seq, D_HYENA)


def _na_bias_index():
    rows = SEQ // GRID_W
    ngrp = rows // NA_QROWS
    qcol = np.arange(GRID_W)
    cstart = np.clip(qcol - NA_WIN_COLS // 2, 0, GRID_W - NA_WIN_COLS)
    ridx = np.zeros((3, NA_QROWS, NA_KROWS), np.int32)
    valid = np.zeros((3, NA_QROWS, GRID_W, NA_KROWS, GRID_W), bool)
    for v, g in enumerate((0, 1, ngrp - 1)):
        u0 = int(np.clip(NA_QROWS * g - NA_WIN_ROWS // 2, 0, rows - NA_KROWS))
        for ri in range(NA_QROWS):
            r = NA_QROWS * g + ri
            rs = int(np.clip(r - NA_WIN_ROWS // 2, 0, rows - NA_WIN_ROWS))
            kr = u0 + np.arange(NA_KROWS)
            row_ok = (kr >= rs) & (kr < rs + NA_WIN_ROWS)
            col_ok = (qcol[None, :] >= cstart[:, None]) & (qcol[None, :] < cstart[:, None] + NA_WIN_COLS)
            ok = row_ok[None, :, None] & col_ok[:, None, :]
            valid[v, ri] = ok
            ridx[v, ri] = np.clip(kr - r + NA_WIN_ROWS - 1, 0, 2 * NA_WIN_ROWS - 2)
    rel = np.clip(qcol[None, :] - qcol[:, None] + NA_WIN_COLS - 1, 0, 2 * NA_WIN_COLS - 2)
    onehot = (rel.reshape(1, -1) == np.arange(2 * NA_WIN_COLS - 1)[:, None]).astype(np.float32)
    nq, nk = NA_QROWS * GRID_W, NA_KROWS * GRID_W
    return ridx, onehot, valid.reshape(3, nq, nk)


def _na_bias(rpb):
    ridx, onehot, valid = _na_bias_index()
    nrow = 2 * NA_WIN_ROWS - 1
    cols = jnp.dot(rpb.reshape(NA_HEADS * nrow, -1), jnp.asarray(onehot), precision=HIGHEST)
    cols = cols.reshape(NA_HEADS, nrow, GRID_W, GRID_W)
    b = cols[:, ridx]
    b = b.transpose(1, 0, 2, 4, 3, 5).reshape(3, NA_HEADS, NA_QROWS * GRID_W, NA_KROWS * GRID_W)
    return jnp.where(valid[:, None], b, NEG_INF)


def _softmax_av(s_list, v_list):
    m = functools.reduce(jnp.maximum, [s.max(axis=-1, keepdims=True) for s in s_list])
    ps = [jnp.exp(s - m) for s in s_list]
    den = functools.reduce(jnp.add, [p.sum(axis=-1, keepdims=True) for p in ps])
    o = functools.reduce(jnp.add, [jnp.dot(p.astype(BF16), v, preferred_element_type=F32) for p, v in zip(ps, v_list)])
    return o / den


def _qk(q, k):
    return lax.dot_general(q, k, (((1,), (1,)), ((), ())), preferred_element_type=F32)


def _na_kernel(q_ref, k_ref, v_ref, kc_ref, vc_ref, bias_ref, prev_ref, o_ref):
    del prev_ref
    rows = SEQ // GRID_W
    g = pl.program_id(1)
    u0 = jnp.clip(NA_QROWS * g - NA_WIN_ROWS // 2, 0, rows - NA_KROWS)
    start = pl.multiple_of(u0 * GRID_W, GRID_W)
    nk = NA_KROWS * GRID_W
    scale = NA_HEAD_DIM ** -0.5
    for h in range(NA_HEADS):
        sl = slice(h * NA_HEAD_DIM, (h + 1) * NA_HEAD_DIM)
        qh = q_ref[:, sl]
        kh = k_ref[pl.ds(start, nk), sl]
        vh = v_ref[pl.ds(start, nk), sl]
        s_loc = _qk(qh, kh) * scale + bias_ref[0, h]
        s_ctx = _qk(qh, kc_ref[:, sl]) * scale
        o_ref[:, sl] = _softmax_av([s_loc, s_ctx], [vh, vc_ref[:, sl]]).astype(o_ref.dtype)


def _na_attention(q, k, v, bias):
    nq = NA_QROWS * GRID_W
    ngrp = SEQ // nq
    ctx0 = N_LAT // CTX_LEN

    def variant(b, g):
        return (jnp.where(g == 0, 0, jnp.where(g == ngrp - 1, 2, 1)), 0, 0, 0)

    return pl.pallas_call(
        _na_kernel,
        grid=(BATCH, ngrp),
        in_specs=[
            pl.BlockSpec((nq, D_NA), lambda b, g: (b * ngrp + g, 0)),
            pl.BlockSpec((SEQ, D_NA), lambda b, g: (b, 0)),
            pl.BlockSpec((SEQ, D_NA), lambda b, g: (b, 0)),
            pl.BlockSpec((CTX_LEN, D_NA), lambda b, g: (ctx0 + b, 0)),
            pl.BlockSpec((CTX_LEN, D_NA), lambda b, g: (ctx0 + b, 0)),
            pl.BlockSpec((1, NA_HEADS, nq, NA_KROWS * GRID_W), variant),
            pl.BlockSpec(memory_space=pl.ANY),
        ],
        out_specs=pl.BlockSpec((nq, D_NA), lambda b, g: (b * ngrp + g, 0)),
        out_shape=jax.ShapeDtypeStruct((N_TOK, D_NA), BF16),
        input_output_aliases={6: 0},
        compiler_params=_cparams(("arbitrary", "arbitrary")),
        name="na_attention",
    )(q, k, v, k, v, bias, jnp.zeros((N_TOK, D_NA), BF16))


def _ctx_attn_kernel(q_ref, k_ref, v_ref, prev_ref, o_ref):
    del prev_ref
    scale = NA_HEAD_DIM ** -0.5
    for h in range(NA_HEADS):
        sl = slice(h * NA_HEAD_DIM, (h + 1) * NA_HEAD_DIM)
        s = _qk(q_ref[:, sl], k_ref[:, sl]) * scale
        o_ref[:, sl] = _softmax_av([s], [v_ref[:, sl]]).astype(o_ref.dtype)


def _ctx_attention(q, k, v, att):
    ctx0 = N_LAT // CTX_LEN
    sp = pl.BlockSpec((CTX_LEN, D_NA), lambda b: (ctx0 + b, 0))
    return pl.pallas_call(
        _ctx_attn_kernel,
        grid=(BATCH,),
        in_specs=[sp, sp, sp, pl.BlockSpec(memory_space=pl.ANY)],
        out_specs=sp,
        out_shape=jax.ShapeDtypeStruct(att.shape, att.dtype),
        input_output_aliases={3: 0},
        compiler_params=_cparams(("arbitrary",)),
        name="ctx_attention",
    )(q, k, v, att)


def _mix_kernel(yc_ref, z_ref, x0_ref, ysc_ref, ya_ref, gt_ref, xl_ref, g1_ref, sh2_ref, sc2_ref, bd_ref,
                wh_ref, ws_ref, wn_ref, wo_ref, lg_ref, lb_ref, xo_ref, ho_ref, hp_ref):
    dot = functools.partial(jnp.dot, preferred_element_type=F32)
    y_hy = x0_ref[...] * (yc_ref[...] + z_ref[...] * bd_ref[...])
    merged = (jax.nn.sigmoid(gt_ref[:, 0:D_MODEL]) * dot(y_hy.astype(BF16), wh_ref[...])
              + jax.nn.sigmoid(gt_ref[:, D_MODEL:2 * D_MODEL]) * dot(ysc_ref[...].astype(BF16), ws_ref[...])
              + jax.nn.sigmoid(gt_ref[:, 2 * D_MODEL:3 * D_MODEL]) * dot(ya_ref[...], wn_ref[...]))
    out = dot(merged.astype(BF16), wo_ref[...])
    xo = _layer_norm(ALPHA * xl_ref[...] + g1_ref[0] * out, lg_ref[...], lb_ref[...])
    xo_ref[...] = xo
    ho = xo * (1.0 + sc2_ref[0]) + sh2_ref[0]
    ho_ref[...] = ho
    hp_ref[...] = _pack_pairs(ho)


def _mix(yc, z, x0, ysc, ya, gates, xl, g1, sh2, sc2, bias_d, wh, ws, wn, wo, ln_g, ln_b, n_tok):
    tm = TOK_TILE
    tok = lambda n: pl.BlockSpec((tm, n), lambda t: (t, 0))
    mod = pl.BlockSpec((1, 1, D_MODEL), lambda t: (_mod_row(t, SEQ // tm, N_LAT // tm), 0, 0))
    const = lambda r, n: pl.BlockSpec((r, n), lambda t: (0, 0))
    return pl.pallas_call(
        _mix_kernel,
        grid=(n_tok // tm,),
        in_specs=[tok(D_HYENA), tok(D_HYENA), tok(D_HYENA), tok(D_SCONV), tok(D_NA), tok(3 * D_MODEL), tok(D_MODEL),
                  mod, mod, mod, const(1, D_HYENA),
                  const(D_HYENA, D_MODEL), const(D_SCONV, D_MODEL), const(D_NA, D_MODEL), const(D_MODEL, D_MODEL),
                  const(1, D_MODEL), const(1, D_MODEL)],
        out_specs=[tok(D_MODEL), tok(D_MODEL), tok(PACK_W)],
        out_shape=[jax.ShapeDtypeStruct((xl.shape[0], D_MODEL), F32)] * 2 + [jax.ShapeDtypeStruct((xl.shape[0], PACK_W), U32)],
        compiler_params=_cparams(("arbitrary",)),
        name="mixer_out",
    )(yc, z, x0, ysc, ya, gates, xl, g1, sh2, sc2, bias_d, wh, ws, wn, wo, ln_g, ln_b)


def _router_kernel(h_ref, wr_ref, rb_ref, tri_ref, te_ref, rk_ref, w_ref, cnt_ref, run_ref):
    tm = h_ref.shape[0]
    per = N_EXPERTS // N_GROUPS

    @pl.when(pl.program_id(0) == 0)
    def _():
        run_ref[...] = jnp.zeros_like(run_ref)

    logits = lax.dot_general(wr_ref[...], h_ref[...], (((1,), (1,)), ((), ())),
                             precision=HIGHEST, preferred_element_type=F32)
    scores = jax.nn.sigmoid(logits)
    sel = scores + rb_ref[...]
    sub = lax.broadcasted_iota(I32, (per, tm), 0)
    colmax = lambda a: jnp.max(a, axis=0, keepdims=True)
    colmin = lambda a: jnp.min(a, axis=0, keepdims=True)
    ninf = -jnp.inf

    xs = [sel[g * per:(g + 1) * per, :] for g in range(N_GROUPS)]
    sc = [scores[g * per:(g + 1) * per, :] for g in range(N_GROUPS)]
    gs = []
    for x in xs:
        m1 = colmax(x)
        i1 = colmin(jnp.where(x == m1, sub, per))
        m2 = colmax(jnp.where(sub == i1, ninf, x))
        gs.append(m1 + m2)
    chosen = [jnp.zeros((1, tm), F32) for _ in range(N_GROUPS)]
    for _ in range(TOPK_GROUPS):
        gm = functools.reduce(jnp.maximum, gs)
        gi = jnp.full((1, tm), N_GROUPS, I32)
        for g in reversed(range(N_GROUPS)):
            gi = jnp.where(gs[g] == gm, g, gi)
        for g in range(N_GROUPS):
            hit = gi == g
            chosen[g] = jnp.where(hit, 1.0, chosen[g])
            gs[g] = jnp.where(hit, ninf, gs[g])
    xm = [jnp.where(jnp.broadcast_to(chosen[g], (per, tm)) > 0.5, xs[g], ninf) for g in range(N_GROUPS)]
    eidx = [sub + g * per for g in range(N_GROUPS)]
    picked = [jnp.zeros((per, tm), F32) for _ in range(N_GROUPS)]
    top_e, top_s = [], []
    for _ in range(TOP_K):
        em = functools.reduce(jnp.maximum, [colmax(x) for x in xm])
        ei = functools.reduce(jnp.minimum, [colmin(jnp.where(xm[g] == em, eidx[g], N_EXPERTS)) for g in range(N_GROUPS)])
        s_acc = jnp.zeros((1, tm), F32)
        for g in range(N_GROUPS):
            hit = eidx[g] == ei
            picked[g] = jnp.where(hit, 1.0, picked[g])
            xm[g] = jnp.where(hit, ninf, xm[g])
            s_acc = s_acc + jnp.sum(jnp.where(hit, sc[g], 0.0), axis=0, keepdims=True)
        top_e.append(ei)
        top_s.append(s_acc)
    den = functools.reduce(jnp.add, top_s)
    mask = jnp.concatenate(picked, axis=0)
    prefix = jnp.dot(mask.astype(BF16), tri_ref[...], preferred_element_type=F32)
    pos = run_ref[:, 0:1] + prefix
    posg = [pos[g * per:(g + 1) * per, :] for g in range(N_GROUPS)]
    for i in range(TOP_K):
        r_acc = jnp.zeros((1, tm), F32)
        for g in range(N_GROUPS):
            r_acc = r_acc + jnp.sum(jnp.where(eidx[g] == top_e[i], posg[g], 0.0), axis=0, keepdims=True)
        te_ref[0, i:i + 1, :] = top_e[i]
        rk_ref[0, i:i + 1, :] = r_acc.astype(I32)
        w_ref[0, i:i + 1, :] = top_s[i] / den * ROUTED_SCALE
    run_ref[...] = run_ref[...] + jnp.sum(mask, axis=1, keepdims=True)
    cnt_ref[...] = run_ref[...]


def _route(tokens, router_w_t, router_b, n_tok):
    tm = RT_TILE
    nt = n_tok // tm
    tri = jnp.asarray(np.triu(np.ones((tm, tm), np.float32), 1), BF16)
    out3 = lambda dt: jax.ShapeDtypeStruct((nt, TOP_K, tm), dt)
    osp = pl.BlockSpec((1, TOP_K, tm), lambda t: (t, 0, 0))
    return pl.pallas_call(
        _router_kernel,
        grid=(nt,),
        in_specs=[
            pl.BlockSpec((tm, D_MODEL), lambda t: (t, 0)),
            pl.BlockSpec((N_EXPERTS, D_MODEL), lambda t: (0, 0)),
            pl.BlockSpec((N_EXPERTS, 1), lambda t: (0, 0)),
            pl.BlockSpec((tm, tm), lambda t: (0, 0)),
        ],
        out_specs=[osp, osp, osp, pl.BlockSpec((N_EXPERTS, LANE), lambda t: (0, 0))],
        out_shape=[out3(I32), out3(I32), out3(F32), jax.ShapeDtypeStruct((N_EXPERTS, LANE), F32)],
        scratch_shapes=[pltpu.VMEM((N_EXPERTS, LANE), F32)],
        compiler_params=_cparams(("arbitrary",)),
        name="moe_router",
    )(tokens, router_w_t, router_b, tri)


def _pack_pairs(v):
    bits = lambda a: lax.bitcast_convert_type(a.astype(BF16).astype(F32), U32)
    return (bits(v[:, PACK_W:]) & jnp.uint32(0xFFFF0000)) | (bits(v[:, :PACK_W]) >> 16)


def _unpack_pairs(w):
    lo = lax.bitcast_convert_type(w << 16, F32)
    hi = lax.bitcast_convert_type(w & jnp.uint32(0xFFFF0000), F32)
    return lo, hi


def _tok_rows(t):
    return pl.ds(t, 1)


def _dispatch_kernel(cnt_ref, pst_ref, na_ref, dest_ref, h_ref, xs_hbm, zero_ref, sem):
    tm = dest_ref.shape[2]
    row_copy = lambda src, dst: pltpu.make_async_copy(src, xs_hbm.at[_tok_rows(dst)], sem)
    blk_rows = MOE_BLOCK * TOK_ROWS
    n_blocks = xs_hbm.shape[0] // blk_rows

    @pl.when(pl.program_id(0) == 0)
    def _():
        zero_ref[...] = jnp.zeros_like(zero_ref)

        def blk_copy(blk):
            return pltpu.make_async_copy(zero_ref, xs_hbm.at[pl.ds(pl.multiple_of(blk * blk_rows, blk_rows), blk_rows)], sem)

        def fill_blk(blk, c):
            blk_copy(blk).start()
            return c

        def drain_blk(blk, c):
            blk_copy(blk).wait()
            return c

        lax.fori_loop(na_ref[0], n_blocks, fill_blk, 0)
        lax.fori_loop(na_ref[0], n_blocks, drain_blk, 0)

        def per_expert(e, carry):
            n = cnt_ref[e]
            n_pad = (n + MOE_BLOCK - 1) // MOE_BLOCK * MOE_BLOCK - n
            base = pst_ref[e] + n

            def fill(r, c):
                row_copy(zero_ref.at[_tok_rows(0)], base + r).start()
                return c

            def drain(r, c):
                row_copy(zero_ref.at[_tok_rows(0)], base).wait()
                return c

            lax.fori_loop(0, n_pad, fill, 0)
            lax.fori_loop(0, n_pad, drain, 0)
            return carry

        lax.fori_loop(0, N_EXPERTS, per_expert, 0)

    def issue(tb, carry):
        for tt in range(ROW_UNROLL):
            t = tb * ROW_UNROLL + tt
            for i in range(TOP_K):
                row_copy(h_ref.at[_tok_rows(t)], dest_ref[0, i, t]).start(priority=i % 2)
        return carry

    def drain(tb, carry):
        for _ in range(WAIT_UNROLL * TOP_K):
            row_copy(h_ref.at[_tok_rows(0)], 0).wait()
        return carry

    lax.fori_loop(0, tm // ROW_UNROLL, issue, 0)
    lax.fori_loop(0, tm // WAIT_UNROLL, drain, 0)


def _dispatch(counts, pstart, n_active, dest, tokens, n_slots):
    nt, _, tm = dest.shape
    grid_spec = pltpu.PrefetchScalarGridSpec(
        num_scalar_prefetch=3,
        grid=(nt,),
        in_specs=[
            pl.BlockSpec((1, TOP_K, tm), lambda t, c, p, a: (t, 0, 0), memory_space=pltpu.SMEM),
            pl.BlockSpec((tm, PACK_W), lambda t, c, p, a: (t, 0)),
        ],
        out_specs=pl.BlockSpec(memory_space=pl.ANY),
        scratch_shapes=[pltpu.VMEM((MOE_BLOCK, PACK_W), U32), pltpu.SemaphoreType.DMA],
    )
    return pl.pallas_call(
        _dispatch_kernel,
        grid_spec=grid_spec,
        out_shape=jax.ShapeDtypeStruct((n_slots, PACK_W), U32),
        compiler_params=_cparams(("arbitrary",)),
        name="moe_dispatch",
    )(counts, pstart, n_active, dest, tokens)


def _expert_kernel(be_ref, na_ref, x_ref, w1_ref, w3_ref, w2_ref, y_ref):
    active = pl.program_id(0) < na_ref[0]

    @pl.when(active)
    def _():
        dot = functools.partial(jnp.dot, preferred_element_type=F32)
        lo, hi = _unpack_pairs(x_ref[...])
        lo, hi = lo.astype(BF16), hi.astype(BF16)
        xw = lambda w_ref: dot(lo, w_ref[0, :PACK_W, :]) + dot(hi, w_ref[0, PACK_W:, :])
        a = xw(w1_ref)
        hidden = (a * jax.nn.sigmoid(a)) * xw(w3_ref)
        y_ref[...] = _pack_pairs(dot(hidden.astype(BF16), w2_ref[0]))

    @pl.when(jnp.logical_not(active))
    def _():
        y_ref[...] = jnp.zeros_like(y_ref)


def _experts(blk_expert, n_active, xs, w1, w3, w2):
    blk_rows = MOE_BLOCK * TOK_ROWS
    n_blocks = xs.shape[0] // blk_rows
    xmap = lambda i, be, na: (jnp.minimum(i, na[0] - 1), 0)
    wmap = lambda i, be, na: (be[i], 0, 0)
    grid_spec = pltpu.PrefetchScalarGridSpec(
        num_scalar_prefetch=2,
        grid=(n_blocks,),
        in_specs=[
            pl.BlockSpec((blk_rows, PACK_W), xmap),
            pl.BlockSpec((1, D_MODEL, D_EXPERT), wmap),
            pl.BlockSpec((1, D_MODEL, D_EXPERT), wmap),
            pl.BlockSpec((1, D_EXPERT, D_MODEL), wmap),
        ],
        out_specs=pl.BlockSpec((blk_rows, PACK_W), lambda i, be, na: (i, 0)),
    )
    return pl.pallas_call(
        _expert_kernel,
        grid_spec=grid_spec,
        out_shape=jax.ShapeDtypeStruct(xs.shape, U32),
        compiler_params=_cparams(("arbitrary",)),
        name="moe_experts",
    )(blk_expert, n_active, xs, w1, w3, w2)


def _combine_kernel(dest_ref, w_ref, h_ref, xl_ref, g2_ref, s1_ref, s3_ref, s2_ref, lg_ref, lb_ref, ys_hbm,
                    xo_ref, ybuf, sem):
    tm = h_ref.shape[0]

    def row_copy(i, t, src):
        return pltpu.make_async_copy(ys_hbm.at[_tok_rows(src)], ybuf.at[i, _tok_rows(t)], sem)

    def issue(tb, carry):
        for tt in range(ROW_UNROLL):
            t = tb * ROW_UNROLL + tt
            for i in range(TOP_K):
                row_copy(i, t, dest_ref[0, i, t]).start(priority=i % 2)
        return carry

    def drain(tb, carry):
        for _ in range(WAIT_UNROLL * TOP_K):
            row_copy(0, 0, 0).wait()
        return carry

    lax.fori_loop(0, tm // ROW_UNROLL, issue, 0)
    dot = functools.partial(jnp.dot, preferred_element_type=F32)
    hb = h_ref[...].astype(BF16)
    a = dot(hb, s1_ref[...])
    f = dot(((a * jax.nn.sigmoid(a)) * dot(hb, s3_ref[...])).astype(BF16), s2_ref[...])
    lax.fori_loop(0, tm // WAIT_UNROLL, drain, 0)
    r_lo = jnp.zeros((tm, PACK_W), F32)
    r_hi = jnp.zeros((tm, PACK_W), F32)
    for i in range(TOP_K):
        lo, hi = _unpack_pairs(ybuf[i])
        r_lo = r_lo + w_ref[:, i:i + 1] * lo
        r_hi = r_hi + w_ref[:, i:i + 1] * hi
    f = f + jnp.concatenate([r_lo, r_hi], axis=-1)
    xo_ref[...] = _layer_norm(ALPHA * xl_ref[...] + g2_ref[0] * f, lg_ref[...], lb_ref[...])


def _combine(dest, w, h, xl, g2, s1, s3, s2, ln_g, ln_b, ys, n_tok):
    tm = CB_TILE
    nt = n_tok // tm
    tok = lambda n: pl.BlockSpec((tm, n), lambda t: (t, 0))
    const = lambda r, n: pl.BlockSpec((r, n), lambda t: (0, 0))
    return pl.pallas_call(
        _combine_kernel,
        grid=(nt,),
        in_specs=[
            pl.BlockSpec((1, TOP_K, tm), lambda t: (t, 0, 0), memory_space=pltpu.SMEM),
            tok(TOP_K), tok(D_MODEL), tok(D_MODEL),
            pl.BlockSpec((1, 1, D_MODEL), lambda t: (_mod_row(t, SEQ // tm, N_LAT // tm), 0, 0)),
            const(D_MODEL, D_SHARED), const(D_MODEL, D_SHARED), const(D_SHARED, D_MODEL),
            const(1, D_MODEL), const(1, D_MODEL),
            pl.BlockSpec(memory_space=pl.ANY),
        ],
        out_specs=tok(D_MODEL),
        out_shape=jax.ShapeDtypeStruct((n_tok, D_MODEL), F32),
        scratch_shapes=[pltpu.VMEM((TOP_K, tm, PACK_W), U32), pltpu.SemaphoreType.DMA],
        compiler_params=_cparams(("arbitrary",)),
        name="moe_combine",
    )(dest, w, h, xl, g2, s1, s3, s2, ln_g, ln_b, ys)


def _moe_layer(h, h_packed, xl, g2, router_w_t, router_b, w1, w3, w2, s1, s3, s2, ln_g, ln_b, n_tok):
    top_e, rank, wts, cnt = _route(h, router_w_t, router_b, n_tok)
    counts = cnt[:, 0].astype(I32)
    padded = (counts + MOE_BLOCK - 1) // MOE_BLOCK * MOE_BLOCK
    pend = jnp.cumsum(padded)
    pstart = pend - padded
    n_blocks = n_tok * TOP_K // MOE_BLOCK + N_EXPERTS
    onehot = top_e[..., None] == jnp.arange(N_EXPERTS, dtype=I32)
    dest = rank + jnp.sum(jnp.where(onehot, pstart, 0), axis=-1)
    blk_expert = jnp.minimum(
        jnp.sum((pend // MOE_BLOCK)[None, :] <= jnp.arange(n_blocks, dtype=I32)[:, None], axis=1), N_EXPERTS - 1).astype(I32)
    n_active = (pend[-1:] // MOE_BLOCK).astype(I32)
    xs = _dispatch(counts, pstart.astype(I32), n_active, dest, h_packed, n_blocks * MOE_BLOCK)
    ys = _experts(blk_expert, n_active, xs, w1, w3, w2)
    nt = n_tok // CB_TILE
    regroup = lambda a: a.transpose(1, 0, 2).reshape(TOP_K, n_tok)
    dest_c = regroup(dest).reshape(TOP_K, nt, CB_TILE).transpose(1, 0, 2)
    w_c = regroup(wts).T
    return _combine(dest_c, w_c, h, xl, g2, s1, s3, s2, ln_g, ln_b, ys, n_tok)


def kernel(x, c, ctx, c_ctx, w_ada, b_ada, w_in, hy_conv_w, hy_conv_b, hy_w1, hy_b1, hy_w2, hy_b2, hy_w3, hy_sin_freq, hy_bias_d, hy_proj, sc_conv_w, sc_proj, na_rpb, na_proj, w_o, ln1_g, ln1_b, ln2_g, ln2_b, moe_router, moe_bias, moe_w1, moe_w3, moe_w2, sh_w1, sh_w3, sh_w2):
    bf = lambda a: a.astype(BF16)
    xa = jnp.concatenate([x.reshape(N_LAT, D_MODEL), ctx.reshape(N_CTX, D_MODEL)], axis=0)

    cc = jnp.zeros((16, D_MODEL), F32).at[:BATCH].set(c).at[BATCH].set(c_ctx)
    mods = _ada(cc, w_ada, b_ada)[:, :BATCH + 1].reshape(DEPTH, BATCH + 1, 6, 1, D_MODEL)
    filt_lat = _hyena_filters(SEQ, hy_w1, hy_b1, hy_w2, hy_b2, hy_w3, hy_sin_freq)
    filt_ctx = _hyena_filters(CTX_LEN, hy_w1, hy_b1, hy_w2, hy_b2, hy_w3, hy_sin_freq)

    offs = [0, 3 * D_HYENA, 3 * D_HYENA + 3 * D_SCONV]
    offs += [offs[2] + D_NA, offs[2] + 2 * D_NA, offs[2] + 3 * D_NA, w_in.shape[2]]

    for i in range(DEPTH):
        last = i == DEPTH - 1
        m = [mods[i, :, j] for j in range(6)]
        w_secs = [bf(w_in[i][:, offs[j]:offs[j + 1]]) for j in range(6)]
        u_hy, u_sc, q, k, v, gates = _proj(xa, m[0], m[1], w_secs, N_TOK)

        z, x0 = _hy_pre(u_hy, hy_conv_w[i], hy_conv_b[i][None], None, SEQ, 0)
        z, x0 = _hy_pre(u_hy, hy_conv_w[i], hy_conv_b[i][None], (z, x0), CTX_LEN, N_LAT)
        yc = jnp.concatenate([_hyena_conv(z, filt_lat[i], SEQ, 0), _hyena_conv(z, filt_ctx[i], CTX_LEN, N_LAT)], axis=0)
        ysc = _short_conv(u_sc, sc_conv_w[i], None, SEQ, 0)
        ysc = _short_conv(u_sc, sc_conv_w[i], ysc, CTX_LEN, N_LAT)
        att = _na_attention(q, k, v, _na_bias(na_rpb[i]))
        att = _ctx_attention(q, k, v, att)

        xa, hmoe, hpk = _mix(yc, z, x0, ysc, att, gates, xa, m[2], m[3], m[4], hy_bias_d[i][None],
                        bf(hy_proj[i]), bf(sc_proj[i]), bf(na_proj[i]), bf(w_o[i]), ln1_g[i][None], ln1_b[i][None], N_TOK)

        n_moe = N_LAT if last else N_TOK
        xa = _moe_layer(hmoe, hpk, xa, m[5], moe_router[i].T, moe_bias[i][:, None], bf(moe_w1[i]), bf(moe_w3[i]),
                        bf(moe_w2[i]), bf(sh_w1[i]), bf(sh_w3[i]), bf(sh_w2[i]), ln2_g[i][None], ln2_b[i][None], n_moe)
    return xa.reshape(BATCH, SEQ, D_MODEL)
```

```python
import functools
import math

import numpy as np
import jax
import jax.numpy as jnp
from jax import lax
from jax.experimental import pallas as pl
from jax.experimental.pallas import tpu as pltpu

F32 = jnp.float32
BF16 = jnp.bfloat16
I32 = jnp.int32
HIGHEST = lax.Precision.HIGHEST

D_MODEL = 1024
BATCH = 8
SEQ = 4096
DEPTH = 4
CTX_LEN = 256
GRID_W = 64
D_HYENA = 512
D_SCONV = 512
NA_HEADS = 8
NA_HEAD_DIM = 64
D_NA = NA_HEADS * NA_HEAD_DIM
NA_WIN_ROWS = 8
NA_WIN_COLS = 16
HY_BANDS = 16
HY_EMB = 1 + 2 * HY_BANDS
HY_FILTER_DIM = 64
HY_FAST_DECAY = 0.3
HY_SLOW_DECAY = 1.5
HY_TARGET = 1e-2
N_EXPERTS = 64
N_GROUPS = 8
TOPK_GROUPS = 4
TOP_K = 8
D_EXPERT = 256
D_SHARED = 256
ROUTED_SCALE = 2.5
LN_EPS = 1e-5
NEG_INF = -1e30
ALPHA = (2.0 * DEPTH) ** 0.25

N_LAT = BATCH * SEQ
N_CTX = BATCH * CTX_LEN
N_TOK = N_LAT + N_CTX

LANE = 128
SUBLANE = 8
VMEM_LIMIT = 56 * 1024 * 1024

TOK_TILE = 256
NA_QROWS = 4
NA_KROWS = NA_QROWS + NA_WIN_ROWS - 1
MOE_BLOCK = 512
RT_TILE = 512
CB_TILE = 128
LC_CH = 8
ROW_UNROLL = 4
WAIT_UNROLL = 16
PACK_W = D_MODEL // 2
TOK_ROWS = PACK_W // LANE
U32 = jnp.uint32


def _cparams(sem):
    return pltpu.CompilerParams(dimension_semantics=sem, vmem_limit_bytes=VMEM_LIMIT)


def _mod_row(tile, tiles_per_batch, n_lat_tiles):
    return jnp.where(tile < n_lat_tiles, tile // tiles_per_batch, BATCH)


def _layer_norm(v, g, b):
    mu = jnp.mean(v, axis=-1, keepdims=True)
    c = v - mu
    var = jnp.mean(c * c, axis=-1, keepdims=True)
    return c * lax.rsqrt(var + LN_EPS) * g + b


def _ada_kernel(c_ref, w_ref, b_ref, o_ref):
    c = c_ref[...]
    cond = c * jax.nn.sigmoid(c)
    o_ref[0] = jnp.dot(cond, w_ref[0], precision=HIGHEST, preferred_element_type=F32) + b_ref[0]


def _ada(cc, w_ada, b_ada):
    depth, d, n = w_ada.shape
    tn = 1536
    return pl.pallas_call(
        _ada_kernel,
        grid=(depth, n // tn),
        in_specs=[
            pl.BlockSpec((16, d), lambda l, j: (0, 0)),
            pl.BlockSpec((1, d, tn), lambda l, j: (l, 0, j)),
            pl.BlockSpec((1, 1, tn), lambda l, j: (l, 0, j)),
        ],
        out_specs=pl.BlockSpec((1, 16, tn), lambda l, j: (l, 0, j)),
        out_shape=jax.ShapeDtypeStruct((depth, 16, n), F32),
        compiler_params=_cparams(("arbitrary", "arbitrary")),
        name="ada",
    )(cc, w_ada, b_ada.reshape(depth, 1, n))


def _proj_kernel(x_ref, sh_ref, sc_ref, w_hy, w_sc, w_q, w_k, w_v, w_g, o_hy, o_sc, o_q, o_k, o_v, o_g):
    h = (x_ref[...] * (1.0 + sc_ref[0]) + sh_ref[0]).astype(BF16)
    for w, o in ((w_hy, o_hy), (w_sc, o_sc), (w_q, o_q), (w_k, o_k), (w_v, o_v), (w_g, o_g)):
        o[...] = jnp.dot(h, w[...], preferred_element_type=F32).astype(o.dtype)


def _proj(x, shift, scale, ws, n_tok):
    tm = TOK_TILE
    nt = n_tok // tm
    mod = lambda t: (_mod_row(t, SEQ // tm, N_LAT // tm), 0, 0)
    widths = [w.shape[1] for w in ws]
    dtypes = [F32, F32, BF16, BF16, BF16, F32]
    return pl.pallas_call(
        _proj_kernel,
        grid=(nt,),
        in_specs=[
            pl.BlockSpec((tm, D_MODEL), lambda t: (t, 0)),
            pl.BlockSpec((1, 1, D_MODEL), mod),
            pl.BlockSpec((1, 1, D_MODEL), mod),
        ] + [pl.BlockSpec((D_MODEL, n), lambda t: (0, 0), pipeline_mode=pl.Buffered(1)) for n in widths],
        out_specs=[pl.BlockSpec((tm, n), lambda t: (t, 0)) for n in widths],
        out_shape=[jax.ShapeDtypeStruct((x.shape[0], n), dt) for n, dt in zip(widths, dtypes)],
        compiler_params=_cparams(("arbitrary",)),
        name="in_proj",
    )(x, shift, scale, *ws)


def _dwconv3(u, w):
    s = u.shape[0]
    row = lax.broadcasted_iota(I32, u.shape, 0)
    prev = jnp.where(row == 0, 0.0, pltpu.roll(u, 1, 0))
    nxt = jnp.where(row == s - 1, 0.0, pltpu.roll(u, s - 1, 0))
    return prev * w[0:1] + u * w[1:2] + nxt * w[2:3]


def _hy_pre_kernel(u0_ref, u1_ref, u2_ref, w0_ref, w1_ref, w2_ref, b0_ref, b1_ref, b2_ref, pz_ref, px_ref,
                   z_ref, x0_ref):
    del pz_ref, px_ref
    x0_ref[...] = _dwconv3(u0_ref[...], w0_ref[...]) + b0_ref[...]
    x1 = _dwconv3(u1_ref[...], w1_ref[...]) + b1_ref[...]
    v = _dwconv3(u2_ref[...], w2_ref[...]) + b2_ref[...]
    z_ref[...] = v * x1


def _hy_pre(u, conv_w, conv_b, prev, seq, row_off):
    ncb = D_HYENA // LANE
    ob = row_off // seq
    usp = lambda s: pl.BlockSpec((seq, LANE), lambda b, c: (ob + b, s * ncb + c))
    wsp = lambda s: pl.BlockSpec((3, LANE), lambda b, c: (0, s * ncb + c))
    bsp = lambda s: pl.BlockSpec((1, LANE), lambda b, c: (0, s * ncb + c))
    osp = pl.BlockSpec((seq, LANE), lambda b, c: (ob + b, c))
    n_tok = u.shape[0]
    if prev is None:
        prev = (jnp.zeros((n_tok, D_HYENA), F32), jnp.zeros((n_tok, D_HYENA), F32))
    args = [u, u, u, conv_w, conv_w, conv_w, conv_b, conv_b, conv_b] + list(prev)
    in_specs = [usp(0), usp(1), usp(2), wsp(0), wsp(1), wsp(2), bsp(0), bsp(1), bsp(2)]
    in_specs += [pl.BlockSpec(memory_space=pl.ANY)] * 2
    return pl.pallas_call(
        _hy_pre_kernel,
        grid=(BATCH, ncb),
        in_specs=in_specs,
        out_specs=[osp, osp],
        out_shape=[jax.ShapeDtypeStruct((n_tok, D_HYENA), F32)] * 2,
        input_output_aliases={9: 0, 10: 1},
        compiler_params=_cparams(("arbitrary", "arbitrary")),
        name="hyena_pre",
    )(*args)


def _sc_kernel(bg_ref, cg_ref, xs_ref, w_ref, prev_ref, o_ref):
    del prev_ref
    o_ref[...] = bg_ref[...] * _dwconv3(cg_ref[...] * xs_ref[...], w_ref[...])


def _short_conv(u, conv_w, prev, seq, row_off):
    ncb = D_SCONV // LANE
    ob = row_off // seq
    usp = lambda s: pl.BlockSpec((seq, LANE), lambda b, c: (ob + b, s * ncb + c))
    osp = pl.BlockSpec((seq, LANE), lambda b, c: (ob + b, c))
    if prev is None:
        prev = jnp.zeros((u.shape[0], D_SCONV), F32)
    in_specs = [usp(0), usp(1), usp(2), pl.BlockSpec((3, LANE), lambda b, c: (0, c)),
                pl.BlockSpec(memory_space=pl.ANY)]
    return pl.pallas_call(
        _sc_kernel,
        grid=(BATCH, ncb),
        in_specs=in_specs,
        out_specs=osp,
        out_shape=jax.ShapeDtypeStruct((u.shape[0], D_SCONV), F32),
        input_output_aliases={4: 0},
        compiler_params=_cparams(("arbitrary", "arbitrary")),
        name="short_conv",
    )(u, u, u, conv_w, prev)


def _filter_kernel(f_ref, dec_ref, w1_ref, b1_ref, w2_ref, b2_ref, w3_ref, fr_ref, o_ref):
    dot = functools.partial(jnp.dot, precision=HIGHEST, preferred_element_type=F32)
    z = jnp.sin(fr_ref[0, 0:1] * (dot(f_ref[...], w1_ref[0]) + b1_ref[0]))
    z = jnp.sin(fr_ref[0, 1:2] * (dot(z, w2_ref[0]) + b2_ref[0]))
    o_ref[0] = dot(z, w3_ref[0]) * dec_ref[...]


def _filter_consts(length):
    t = jnp.linspace(0.0, 1.0, length, dtype=F32)[:, None]
    ang = (2.0 * math.pi / length) * jnp.arange(length, dtype=F32)[:, None]
    bands = jnp.linspace(1e-4, HY_BANDS - 1, HY_BANDS, dtype=F32)[None, :]
    feats = jnp.concatenate([t, jnp.cos(bands * ang), -jnp.sin(bands * ang)], -1)
    feats = jnp.pad(feats, ((0, 0), (0, LANE - HY_EMB)))
    deltas = jnp.abs(jnp.linspace(math.log(HY_TARGET) / HY_SLOW_DECAY, math.log(HY_TARGET) / HY_FAST_DECAY,
                                  D_HYENA, dtype=F32))
    decay = jnp.exp(-t * deltas[None, :])
    return feats, jnp.concatenate([decay, decay], -1)


def _hyena_filters(length, w1, b1, w2, b2, w3, sin_freq):
    depth = w1.shape[0]
    pf = LANE - HY_FILTER_DIM
    w1p = jnp.pad(w1, ((0, 0), (0, LANE - HY_EMB), (0, pf)))
    b1p = jnp.pad(b1, ((0, 0), (0, pf)))[:, None, :]
    w2p = jnp.pad(w2, ((0, 0), (0, pf), (0, pf)))
    b2p = jnp.pad(b2, ((0, 0), (0, pf)))[:, None, :]
    w3p = jnp.pad(w3, ((0, 0), (0, pf), (0, 0)))
    frp = jnp.pad(sin_freq, ((0, 0), (0, 0), (0, pf)))
    feats, decay = _filter_consts(length)
    tl = min(length, 512)
    lsp = lambda shape: pl.BlockSpec((1,) + shape, lambda l, j: (l, 0, 0))
    h = pl.pallas_call(
        _filter_kernel,
        grid=(depth, length // tl),
        in_specs=[
            pl.BlockSpec((tl, LANE), lambda l, j: (j, 0)),
            pl.BlockSpec((tl, 2 * D_HYENA), lambda l, j: (j, 0)),
            lsp((LANE, LANE)), lsp((1, LANE)), lsp((LANE, LANE)), lsp((1, LANE)),
            lsp((LANE, 2 * D_HYENA)), lsp((2, LANE)),
        ],
        out_specs=pl.BlockSpec((1, tl, 2 * D_HYENA), lambda l, j: (l, j, 0)),
        out_shape=jax.ShapeDtypeStruct((depth, length, 2 * D_HYENA), F32),
        compiler_params=_cparams(("arbitrary", "arbitrary")),
        name="hyena_filter",
    )(feats, decay, w1p, b1p, w2p, b2p, w3p, frp)
    h_fwd = h[:, :, :D_HYENA]
    h_bwd = h[:, :, D_HYENA:]
    g_lin = jnp.concatenate([jnp.zeros((depth, 1, D_HYENA), F32), h_bwd[:, :0:-1], h_fwd], axis=1)
    return g_lin.transpose(0, 2, 1).reshape(depth, D_HYENA, 2 * length // LANE, LANE)


def _lconv_kernel(nb, z_ref, g_ref, y_ref, zl_ref):
    krow = lax.broadcasted_iota(I32, (LANE, LANE), 0)
    acol = lax.broadcasted_iota(I32, (LANE, LANE), 1)
    upper = acol >= krow

    def shifted(c, seg):
        return pltpu.roll(jnp.broadcast_to(g_ref[c, seg:seg + 1, :], (LANE, LANE)), 0, 1, stride=1, stride_axis=0)

    def per_channel(c, carry):
        for j in range(nb):
            zl_ref[j * BATCH:(j + 1) * BATCH, :] = z_ref[c, :, j * LANE:(j + 1) * LANE]
        y_ref[c] = jnp.zeros((nb * BATCH, LANE), F32)
        prev = shifted(c, 0)
        for d in range(1 - nb, nb):
            cur = shifted(c, d + nb)
            toep = jnp.where(upper, cur, prev).astype(BF16)
            j0, j1 = max(0, -d), min(nb, nb - d)
            part = jnp.dot(zl_ref[j0 * BATCH:j1 * BATCH, :].astype(BF16), toep, preferred_element_type=F32)
            y_ref[c, (j0 + d) * BATCH:(j1 + d) * BATCH, :] += part
            prev = cur
        return carry

    lax.fori_loop(0, LC_CH, per_channel, 0)


def _long_conv(zt, g):
    ch, _, length = zt.shape
    nb = length // LANE
    return pl.pallas_call(
        functools.partial(_lconv_kernel, nb),
        grid=(ch // LC_CH,),
        in_specs=[
            pl.BlockSpec((LC_CH, BATCH, length), lambda i: (i, 0, 0)),
            pl.BlockSpec((LC_CH, 2 * nb, LANE), lambda i: (i, 0, 0)),
        ],
        out_specs=pl.BlockSpec((LC_CH, nb * BATCH, LANE), lambda i: (i, 0, 0)),
        out_shape=jax.ShapeDtypeStruct((ch, nb * BATCH, LANE), F32),
        scratch_shapes=[pltpu.VMEM((nb * BATCH, LANE), F32)],
        compiler_params=_cparams(("arbitrary",)),
        name="hyena_long_conv",
    )(zt, g)


def _hyena_conv(z, g, seq, row_off):
    zt = z[row_off:row_off + BATCH * seq].reshape(BATCH, seq, D_HYENA).transpose(2, 0, 1)
    yt = _long_conv(zt, g)
    nb = seq // LANE
    return yt.reshape(D_HYENA, nb, BATCH, LANE).transpose(2, 1, 3, 0).reshape(BATCH * seq, D_HYENA)


def _na_bias_index():
    rows = SEQ // GRID_W
    ngrp = rows // NA_QROWS
    qcol = np.arange(GRID_W)
    cstart = np.clip(qcol - NA_WIN_COLS // 2, 0, GRID_W - NA_WIN_COLS)
    ridx = np.zeros((3, NA_QROWS, NA_KROWS), np.int32)
    valid = np.zeros((3, NA_QROWS, GRID_W, NA_KROWS, GRID_W), bool)
    for v, g in enumerate((0, 1, ngrp - 1)):
        u0 = int(np.clip(NA_QROWS * g - NA_WIN_ROWS // 2, 0, rows - NA_KROWS))
        for ri in range(NA_QROWS):
            r = NA_QROWS * g + ri
            rs = int(np.clip(r - NA_WIN_ROWS // 2, 0, rows - NA_WIN_ROWS))
            kr = u0 + np.arange(NA_KROWS)
            row_ok = (kr >= rs) & (kr < rs + NA_WIN_ROWS)
            col_ok = (qcol[None, :] >= cstart[:, None]) & (qcol[None, :] < cstart[:, None] + NA_WIN_COLS)
            ok = row_ok[None, :, None] & col_ok[:, None, :]
            valid[v, ri] = ok
            ridx[v, ri] = np.clip(kr - r + NA_WIN_ROWS - 1, 0, 2 * NA_WIN_ROWS - 2)
    rel = np.clip(qcol[None, :] - qcol[:, None] + NA_WIN_COLS - 1, 0, 2 * NA_WIN_COLS - 2)
    onehot = (rel.reshape(1, -1) == np.arange(2 * NA_WIN_COLS - 1)[:, None]).astype(np.float32)
    nq, nk = NA_QROWS * GRID_W, NA_KROWS * GRID_W
    return ridx, onehot, valid.reshape(3, nq, nk)


def _na_bias(rpb):
    ridx, onehot, valid = _na_bias_index()
    nrow = 2 * NA_WIN_ROWS - 1
    cols = jnp.dot(rpb.reshape(NA_HEADS * nrow, -1), jnp.asarray(onehot), precision=HIGHEST)
    cols = cols.reshape(NA_HEADS, nrow, GRID_W, GRID_W)
    b = cols[:, ridx]
    b = b.transpose(1, 0, 2, 4, 3, 5).reshape(3, NA_HEADS, NA_QROWS * GRID_W, NA_KROWS * GRID_W)
    return jnp.where(valid[:, None], b, NEG_INF)


def _softmax_av(s_list, v_list):
    m = functools.reduce(jnp.maximum, [s.max(axis=-1, keepdims=True) for s in s_list])
    ps = [jnp.exp(s - m) for s in s_list]
    den = functools.reduce(jnp.add, [p.sum(axis=-1, keepdims=True) for p in ps])
    o = functools.reduce(jnp.add, [jnp.dot(p.astype(BF16), v, preferred_element_type=F32) for p, v in zip(ps, v_list)])
    return o / den


def _qk(q, k):
    return lax.dot_general(q, k, (((1,), (1,)), ((), ())), preferred_element_type=F32)


def _na_kernel(q_ref, k_ref, v_ref, kc_ref, vc_ref, bias_ref, prev_ref, o_ref):
    del prev_ref
    rows = SEQ // GRID_W
    g = pl.program_id(1)
    u0 = jnp.clip(NA_QROWS * g - NA_WIN_ROWS // 2, 0, rows - NA_KROWS)
    start = pl.multiple_of(u0 * GRID_W, GRID_W)
    nk = NA_KROWS * GRID_W
    scale = NA_HEAD_DIM ** -0.5
    for h in range(NA_HEADS):
        sl = slice(h * NA_HEAD_DIM, (h + 1) * NA_HEAD_DIM)
        qh = q_ref[:, sl]
        kh = k_ref[pl.ds(start, nk), sl]
        vh = v_ref[pl.ds(start, nk), sl]
        s_loc = _qk(qh, kh) * scale + bias_ref[0, h]
        s_ctx = _qk(qh, kc_ref[:, sl]) * scale
        o_ref[:, sl] = _softmax_av([s_loc, s_ctx], [vh, vc_ref[:, sl]]).astype(o_ref.dtype)


def _na_attention(q, k, v, bias):
    nq = NA_QROWS * GRID_W
    ngrp = SEQ // nq
    ctx0 = N_LAT // CTX_LEN

    def variant(b, g):
        return (jnp.where(g == 0, 0, jnp.where(g == ngrp - 1, 2, 1)), 0, 0, 0)

    return pl.pallas_call(
        _na_kernel,
        grid=(BATCH, ngrp),
        in_specs=[
            pl.BlockSpec((nq, D_NA), lambda b, g: (b * ngrp + g, 0)),
            pl.BlockSpec((SEQ, D_NA), lambda b, g: (b, 0)),
            pl.BlockSpec((SEQ, D_NA), lambda b, g: (b, 0)),
            pl.BlockSpec((CTX_LEN, D_NA), lambda b, g: (ctx0 + b, 0)),
            pl.BlockSpec((CTX_LEN, D_NA), lambda b, g: (ctx0 + b, 0)),
            pl.BlockSpec((1, NA_HEADS, nq, NA_KROWS * GRID_W), variant),
            pl.BlockSpec(memory_space=pl.ANY),
        ],
        out_specs=pl.BlockSpec((nq, D_NA), lambda b, g: (b * ngrp + g, 0)),
        out_shape=jax.ShapeDtypeStruct((N_TOK, D_NA), BF16),
        input_output_aliases={6: 0},
        compiler_params=_cparams(("arbitrary", "arbitrary")),
        name="na_attention",
    )(q, k, v, k, v, bias, jnp.zeros((N_TOK, D_NA), BF16))


def _ctx_attn_kernel(q_ref, k_ref, v_ref, prev_ref, o_ref):
    del prev_ref
    scale = NA_HEAD_DIM ** -0.5
    for h in range(NA_HEADS):
        sl = slice(h * NA_HEAD_DIM, (h + 1) * NA_HEAD_DIM)
        s = _qk(q_ref[:, sl], k_ref[:, sl]) * scale
        o_ref[:, sl] = _softmax_av([s], [v_ref[:, sl]]).astype(o_ref.dtype)


def _ctx_attention(q, k, v, att):
    ctx0 = N_LAT // CTX_LEN
    sp = pl.BlockSpec((CTX_LEN, D_NA), lambda b: (ctx0 + b, 0))
    return pl.pallas_call(
        _ctx_attn_kernel,
        grid=(BATCH,),
        in_specs=[sp, sp, sp, pl.BlockSpec(memory_space=pl.ANY)],
        out_specs=sp,
        out_shape=jax.ShapeDtypeStruct(att.shape, att.dtype),
        input_output_aliases={3: 0},
        compiler_params=_cparams(("arbitrary",)),
        name="ctx_attention",
    )(q, k, v, att)


def _mix_kernel(yc_ref, z_ref, x0_ref, ysc_ref, ya_ref, gt_ref, xl_ref, g1_ref, sh2_ref, sc2_ref, bd_ref,
                wh_ref, ws_ref, wn_ref, wo_ref, lg_ref, lb_ref, xo_ref, ho_ref, hp_ref):
    dot = functools.partial(jnp.dot, preferred_element_type=F32)
    y_hy = x0_ref[...] * (yc_ref[...] + z_ref[...] * bd_ref[...])
    merged = (jax.nn.sigmoid(gt_ref[:, 0:D_MODEL]) * dot(y_hy.astype(BF16), wh_ref[...])
              + jax.nn.sigmoid(gt_ref[:, D_MODEL:2 * D_MODEL]) * dot(ysc_ref[...].astype(BF16), ws_ref[...])
              + jax.nn.sigmoid(gt_ref[:, 2 * D_MODEL:3 * D_MODEL]) * dot(ya_ref[...], wn_ref[...]))
    out = dot(merged.astype(BF16), wo_ref[...])
    xo = _layer_norm(ALPHA * xl_ref[...] + g1_ref[0] * out, lg_ref[...], lb_ref[...])
    xo_ref[...] = xo
    ho = xo * (1.0 + sc2_ref[0]) + sh2_ref[0]
    ho_ref[...] = ho
    _store_words(hp_ref, _pack_pairs(ho), ho.shape[0])


def _mix(yc, z, x0, ysc, ya, gates, xl, g1, sh2, sc2, bias_d, wh, ws, wn, wo, ln_g, ln_b, n_tok):
    tm = TOK_TILE
    tok = lambda n: pl.BlockSpec((tm, n), lambda t: (t, 0))
    mod = pl.BlockSpec((1, 1, D_MODEL), lambda t: (_mod_row(t, SEQ // tm, N_LAT // tm), 0, 0))
    const = lambda r, n: pl.BlockSpec((r, n), lambda t: (0, 0))
    return pl.pallas_call(
        _mix_kernel,
        grid=(n_tok // tm,),
        in_specs=[tok(D_HYENA), tok(D_HYENA), tok(D_HYENA), tok(D_SCONV), tok(D_NA), tok(3 * D_MODEL), tok(D_MODEL),
                  mod, mod, mod, const(1, D_HYENA),
                  const(D_HYENA, D_MODEL), const(D_SCONV, D_MODEL), const(D_NA, D_MODEL), const(D_MODEL, D_MODEL),
                  const(1, D_MODEL), const(1, D_MODEL)],
        out_specs=[tok(D_MODEL), tok(D_MODEL), pl.BlockSpec((tm * TOK_ROWS, LANE), lambda t: (t, 0))],
        out_shape=[jax.ShapeDtypeStruct((xl.shape[0], D_MODEL), F32)] * 2
        + [jax.ShapeDtypeStruct((xl.shape[0] * TOK_ROWS, LANE), U32)],
        compiler_params=_cparams(("arbitrary",)),
        name="mixer_out",
    )(yc, z, x0, ysc, ya, gates, xl, g1, sh2, sc2, bias_d, wh, ws, wn, wo, ln_g, ln_b)


def _router_kernel(h_ref, wr_ref, rb_ref, tri_ref, te_ref, rk_ref, w_ref, cnt_ref, run_ref):
    tm = h_ref.shape[0]
    per = N_EXPERTS // N_GROUPS

    @pl.when(pl.program_id(0) == 0)
    def _():
        run_ref[...] = jnp.zeros_like(run_ref)

    logits = lax.dot_general(wr_ref[...], h_ref[...], (((1,), (1,)), ((), ())),
                             precision=HIGHEST, preferred_element_type=F32)
    scores = jax.nn.sigmoid(logits)
    sel = scores + rb_ref[...]
    sub = lax.broadcasted_iota(I32, (per, tm), 0)
    colmax = lambda a: jnp.max(a, axis=0, keepdims=True)
    colmin = lambda a: jnp.min(a, axis=0, keepdims=True)
    ninf = -jnp.inf

    xs = [sel[g * per:(g + 1) * per, :] for g in range(N_GROUPS)]
    sc = [scores[g * per:(g + 1) * per, :] for g in range(N_GROUPS)]
    gs = []
    for x in xs:
        m1 = colmax(x)
        i1 = colmin(jnp.where(x == m1, sub, per))
        m2 = colmax(jnp.where(sub == i1, ninf, x))
        gs.append(m1 + m2)
    chosen = [jnp.zeros((1, tm), F32) for _ in range(N_GROUPS)]
    for _ in range(TOPK_GROUPS):
        gm = functools.reduce(jnp.maximum, gs)
        gi = jnp.full((1, tm), N_GROUPS, I32)
        for g in reversed(range(N_GROUPS)):
            gi = jnp.where(gs[g] == gm, g, gi)
        for g in range(N_GROUPS):
            hit = gi == g
            chosen[g] = jnp.where(hit, 1.0, chosen[g])
            gs[g] = jnp.where(hit, ninf, gs[g])
    xm = [jnp.where(jnp.broadcast_to(chosen[g], (per, tm)) > 0.5, xs[g], ninf) for g in range(N_GROUPS)]
    eidx = [sub + g * per for g in range(N_GROUPS)]
    picked = [jnp.zeros((per, tm), F32) for _ in range(N_GROUPS)]
    top_e, top_s = [], []
    for _ in range(TOP_K):
        em = functools.reduce(jnp.maximum, [colmax(x) for x in xm])
        ei = functools.reduce(jnp.minimum, [colmin(jnp.where(xm[g] == em, eidx[g], N_EXPERTS)) for g in range(N_GROUPS)])
        s_acc = jnp.zeros((1, tm), F32)
        for g in range(N_GROUPS):
            hit = eidx[g] == ei
            picked[g] = jnp.where(hit, 1.0, picked[g])
            xm[g] = jnp.where(hit, ninf, xm[g])
            s_acc = s_acc + jnp.sum(jnp.where(hit, sc[g], 0.0), axis=0, keepdims=True)
        top_e.append(ei)
        top_s.append(s_acc)
    den = functools.reduce(jnp.add, top_s)
    mask = jnp.concatenate(picked, axis=0)
    prefix = jnp.dot(mask.astype(BF16), tri_ref[...], preferred_element_type=F32)
    pos = run_ref[:, 0:1] + prefix
    posg = [pos[g * per:(g + 1) * per, :] for g in range(N_GROUPS)]
    for i in range(TOP_K):
        r_acc = jnp.zeros((1, tm), F32)
        for g in range(N_GROUPS):
            r_acc = r_acc + jnp.sum(jnp.where(eidx[g] == top_e[i], posg[g], 0.0), axis=0, keepdims=True)
        te_ref[0, i:i + 1, :] = top_e[i]
        rk_ref[0, i:i + 1, :] = r_acc.astype(I32)
        w_ref[0, i:i + 1, :] = top_s[i] / den * ROUTED_SCALE
    run_ref[...] = run_ref[...] + jnp.sum(mask, axis=1, keepdims=True)
    cnt_ref[...] = run_ref[...]


def _route(tokens, router_w_t, router_b, n_tok):
    tm = RT_TILE
    nt = n_tok // tm
    tri = jnp.asarray(np.triu(np.ones((tm, tm), np.float32), 1), BF16)
    out3 = lambda dt: jax.ShapeDtypeStruct((nt, TOP_K, tm), dt)
    osp = pl.BlockSpec((1, TOP_K, tm), lambda t: (t, 0, 0))
    return pl.pallas_call(
        _router_kernel,
        grid=(nt,),
        in_specs=[
            pl.BlockSpec((tm, D_MODEL), lambda t: (t, 0)),
            pl.BlockSpec((N_EXPERTS, D_MODEL), lambda t: (0, 0)),
            pl.BlockSpec((N_EXPERTS, 1), lambda t: (0, 0)),
            pl.BlockSpec((tm, tm), lambda t: (0, 0)),
        ],
        out_specs=[osp, osp, osp, pl.BlockSpec((N_EXPERTS, LANE), lambda t: (0, 0))],
        out_shape=[out3(I32), out3(I32), out3(F32), jax.ShapeDtypeStruct((N_EXPERTS, LANE), F32)],
        scratch_shapes=[pltpu.VMEM((N_EXPERTS, LANE), F32)],
        compiler_params=_cparams(("arbitrary",)),
        name="moe_router",
    )(tokens, router_w_t, router_b, tri)


def _pack_pairs(v):
    bits = lambda a: lax.bitcast_convert_type(a.astype(BF16).astype(F32), U32)
    return (bits(v[:, PACK_W:]) & jnp.uint32(0xFFFF0000)) | (bits(v[:, :PACK_W]) >> 16)


def _unpack_pairs(w):
    lo = lax.bitcast_convert_type(w << 16, F32)
    hi = lax.bitcast_convert_type(w & jnp.uint32(0xFFFF0000), F32)
    return lo, hi


def _tok_rows(t):
    return pl.ds(pl.multiple_of(t * TOK_ROWS, TOK_ROWS), TOK_ROWS)


def _load_words(ref, n, lead=()):
    return jnp.concatenate([ref[lead + (pl.ds(j, n, stride=TOK_ROWS), slice(None))] for j in range(TOK_ROWS)], axis=-1)


def _store_words(ref, words, n):
    for j in range(TOK_ROWS):
        ref[pl.ds(j, n, stride=TOK_ROWS), :] = words[:, j * LANE:(j + 1) * LANE]


def _dispatch_kernel(cnt_ref, pst_ref, na_ref, dest_ref, h_ref, xs_hbm, zero_ref, sem):
    tm = dest_ref.shape[2]
    row_copy = lambda src, dst: pltpu.make_async_copy(src, xs_hbm.at[_tok_rows(dst)], sem)
    blk_rows = MOE_BLOCK * TOK_ROWS
    n_blocks = xs_hbm.shape[0] // blk_rows

    @pl.when(pl.program_id(0) == 0)
    def _():
        zero_ref[...] = jnp.zeros_like(zero_ref)

        def blk_copy(blk):
            return pltpu.make_async_copy(zero_ref, xs_hbm.at[pl.ds(pl.multiple_of(blk * blk_rows, blk_rows), blk_rows)], sem)

        def fill_blk(blk, c):
            blk_copy(blk).start()
            return c

        def drain_blk(blk, c):
            blk_copy(blk).wait()
            return c

        lax.fori_loop(na_ref[0], n_blocks, fill_blk, 0)
        lax.fori_loop(na_ref[0], n_blocks, drain_blk, 0)

        def per_expert(e, carry):
            n = cnt_ref[e]
            n_pad = (n + MOE_BLOCK - 1) // MOE_BLOCK * MOE_BLOCK - n
            base = pst_ref[e] + n

            def fill(r, c):
                row_copy(zero_ref.at[_tok_rows(0)], base + r).start()
                return c

            def drain(r, c):
                row_copy(zero_ref.at[_tok_rows(0)], base).wait()
                return c

            lax.fori_loop(0, n_pad, fill, 0)
            lax.fori_loop(0, n_pad, drain, 0)
            return carry

        lax.fori_loop(0, N_EXPERTS, per_expert, 0)

    def issue(tb, carry):
        for tt in range(ROW_UNROLL):
            t = tb * ROW_UNROLL + tt
            for i in range(TOP_K):
                row_copy(h_ref.at[_tok_rows(t)], dest_ref[0, i, t]).start(priority=i % 2)
        return carry

    def drain(tb, carry):
        for _ in range(WAIT_UNROLL * TOP_K):
            row_copy(h_ref.at[_tok_rows(0)], 0).wait()
        return carry

    lax.fori_loop(0, tm // ROW_UNROLL, issue, 0)
    lax.fori_loop(0, tm // WAIT_UNROLL, drain, 0)


def _dispatch(counts, pstart, n_active, dest, tokens, n_slots):
    nt, _, tm = dest.shape
    grid_spec = pltpu.PrefetchScalarGridSpec(
        num_scalar_prefetch=3,
        grid=(nt,),
        in_specs=[
            pl.BlockSpec((1, TOP_K, tm), lambda t, c, p, a: (t, 0, 0), memory_space=pltpu.SMEM),
            pl.BlockSpec((tm * TOK_ROWS, LANE), lambda t, c, p, a: (t, 0)),
        ],
        out_specs=pl.BlockSpec(memory_space=pl.ANY),
        scratch_shapes=[pltpu.VMEM((MOE_BLOCK * TOK_ROWS, LANE), U32), pltpu.SemaphoreType.DMA],
    )
    return pl.pallas_call(
        _dispatch_kernel,
        grid_spec=grid_spec,
        out_shape=jax.ShapeDtypeStruct((n_slots * TOK_ROWS, LANE), U32),
        compiler_params=_cparams(("arbitrary",)),
        name="moe_dispatch",
    )(counts, pstart, n_active, dest, tokens)


def _expert_kernel(be_ref, na_ref, x_ref, w1_ref, w3_ref, w2_ref, y_ref):
    active = pl.program_id(0) < na_ref[0]

    @pl.when(active)
    def _():
        dot = functools.partial(jnp.dot, preferred_element_type=F32)
        lo, hi = _unpack_pairs(_load_words(x_ref, MOE_BLOCK))
        lo, hi = lo.astype(BF16), hi.astype(BF16)
        xw = lambda w_ref: dot(lo, w_ref[0, :PACK_W, :]) + dot(hi, w_ref[0, PACK_W:, :])
        a = xw(w1_ref)
        hidden = (a * jax.nn.sigmoid(a)) * xw(w3_ref)
        _store_words(y_ref, _pack_pairs(dot(hidden.astype(BF16), w2_ref[0])), MOE_BLOCK)

    @pl.when(jnp.logical_not(active))
    def _():
        y_ref[...] = jnp.zeros_like(y_ref)


def _experts(blk_expert, n_active, xs, w1, w3, w2):
    blk_rows = MOE_BLOCK * TOK_ROWS
    n_blocks = xs.shape[0] // blk_rows
    xmap = lambda i, be, na: (jnp.minimum(i, na[0] - 1), 0)
    wmap = lambda i, be, na: (be[i], 0, 0)
    grid_spec = pltpu.PrefetchScalarGridSpec(
        num_scalar_prefetch=2,
        grid=(n_blocks,),
        in_specs=[
            pl.BlockSpec((blk_rows, LANE), xmap),
            pl.BlockSpec((1, D_MODEL, D_EXPERT), wmap),
            pl.BlockSpec((1, D_MODEL, D_EXPERT), wmap),
            pl.BlockSpec((1, D_EXPERT, D_MODEL), wmap),
        ],
        out_specs=pl.BlockSpec((blk_rows, LANE), lambda i, be, na: (i, 0)),
    )
    return pl.pallas_call(
        _expert_kernel,
        grid_spec=grid_spec,
        out_shape=jax.ShapeDtypeStruct(xs.shape, U32),
        compiler_params=_cparams(("arbitrary",)),
        name="moe_experts",
    )(blk_expert, n_active, xs, w1, w3, w2)


def _combine_kernel(dest_ref, w_ref, h_ref, xl_ref, g2_ref, s1_ref, s3_ref, s2_ref, lg_ref, lb_ref, ys_hbm,
                    xo_ref, ybuf, sem):
    tm = h_ref.shape[0]

    def row_copy(i, t, src):
        return pltpu.make_async_copy(ys_hbm.at[_tok_rows(src)], ybuf.at[i, _tok_rows(t)], sem)

    def issue(tb, carry):
        for tt in range(ROW_UNROLL):
            t = tb * ROW_UNROLL + tt
            for i in range(TOP_K):
                row_copy(i, t, dest_ref[0, i, t]).start(priority=i % 2)
        return carry

    def drain(tb, carry):
        for _ in range(WAIT_UNROLL * TOP_K):
            row_copy(0, 0, 0).wait()
        return carry

    lax.fori_loop(0, tm // ROW_UNROLL, issue, 0)
    dot = functools.partial(jnp.dot, preferred_element_type=F32)
    hb = h_ref[...].astype(BF16)
    a = dot(hb, s1_ref[...])
    f = dot(((a * jax.nn.sigmoid(a)) * dot(hb, s3_ref[...])).astype(BF16), s2_ref[...])
    lax.fori_loop(0, tm // WAIT_UNROLL, drain, 0)
    r_lo = jnp.zeros((tm, PACK_W), F32)
    r_hi = jnp.zeros((tm, PACK_W), F32)
    for i in range(TOP_K):
        lo, hi = _unpack_pairs(_load_words(ybuf, tm, (i,)))
        r_lo = r_lo + w_ref[:, i:i + 1] * lo
        r_hi = r_hi + w_ref[:, i:i + 1] * hi
    f = f + jnp.concatenate([r_lo, r_hi], axis=-1)
    xo_ref[...] = _layer_norm(ALPHA * xl_ref[...] + g2_ref[0] * f, lg_ref[...], lb_ref[...])


def _combine(dest, w, h, xl, g2, s1, s3, s2, ln_g, ln_b, ys, n_tok):
    tm = CB_TILE
    nt = n_tok // tm
    tok = lambda n: pl.BlockSpec((tm, n), lambda t: (t, 0))
    const = lambda r, n: pl.BlockSpec((r, n), lambda t: (0, 0))
    return pl.pallas_call(
        _combine_kernel,
        grid=(nt,),
        in_specs=[
            pl.BlockSpec((1, TOP_K, tm), lambda t: (t, 0, 0), memory_space=pltpu.SMEM),
            tok(TOP_K), tok(D_MODEL), tok(D_MODEL),
            pl.BlockSpec((1, 1, D_MODEL), lambda t: (_mod_row(t, SEQ // tm, N_LAT // tm), 0, 0)),
            const(D_MODEL, D_SHARED), const(D_MODEL, D_SHARED), const(D_SHARED, D_MODEL),
            const(1, D_MODEL), const(1, D_MODEL),
            pl.BlockSpec(memory_space=pl.ANY),
        ],
        out_specs=tok(D_MODEL),
        out_shape=jax.ShapeDtypeStruct((n_tok, D_MODEL), F32),
        scratch_shapes=[pltpu.VMEM((TOP_K, tm * TOK_ROWS, LANE), U32), pltpu.SemaphoreType.DMA],
        compiler_params=_cparams(("arbitrary",)),
        name="moe_combine",
    )(dest, w, h, xl, g2, s1, s3, s2, ln_g, ln_b, ys)


def _moe_layer(h, h_packed, xl, g2, router_w_t, router_b, w1, w3, w2, s1, s3, s2, ln_g, ln_b, n_tok):
    top_e, rank, wts, cnt = _route(h, router_w_t, router_b, n_tok)
    counts = cnt[:, 0].astype(I32)
    padded = (counts + MOE_BLOCK - 1) // MOE_BLOCK * MOE_BLOCK
    pend = jnp.cumsum(padded)
    pstart = pend - padded
    n_blocks = n_tok * TOP_K // MOE_BLOCK + N_EXPERTS
    onehot = top_e[..., None] == jnp.arange(N_EXPERTS, dtype=I32)
    dest = rank + jnp.sum(jnp.where(onehot, pstart, 0), axis=-1)
    blk_expert = jnp.minimum(
        jnp.sum((pend // MOE_BLOCK)[None, :] <= jnp.arange(n_blocks, dtype=I32)[:, None], axis=1), N_EXPERTS - 1).astype(I32)
    n_active = (pend[-1:] // MOE_BLOCK).astype(I32)
    xs = _dispatch(counts, pstart.astype(I32), n_active, dest, h_packed, n_blocks * MOE_BLOCK)
    ys = _experts(blk_expert, n_active, xs, w1, w3, w2)
    nt = n_tok // CB_TILE
    regroup = lambda a: a.transpose(1, 0, 2).reshape(TOP_K, n_tok)
    dest_c = regroup(dest).reshape(TOP_K, nt, CB_TILE).transpose(1, 0, 2)
    w_c = regroup(wts).T
    return _combine(dest_c, w_c, h, xl, g2, s1, s3, s2, ln_g, ln_b, ys, n_tok)


def kernel(x, c, ctx, c_ctx, w_ada, b_ada, w_in, hy_conv_w, hy_conv_b, hy_w1, hy_b1, hy_w2, hy_b2, hy_w3, hy_sin_freq, hy_bias_d, hy_proj, sc_conv_w, sc_proj, na_rpb, na_proj, w_o, ln1_g, ln1_b, ln2_g, ln2_b, moe_router, moe_bias, moe_w1, moe_w3, moe_w2, sh_w1, sh_w3, sh_w2):
    bf = lambda a: a.astype(BF16)
    xa = jnp.concatenate([x.reshape(N_LAT, D_MODEL), ctx.reshape(N_CTX, D_MODEL)], axis=0)

    cc = jnp.zeros((16, D_MODEL), F32).at[:BATCH].set(c).at[BATCH].set(c_ctx)
    mods = _ada(cc, w_ada, b_ada)[:, :BATCH + 1].reshape(DEPTH, BATCH + 1, 6, 1, D_MODEL)
    filt_lat = _hyena_filters(SEQ, hy_w1, hy_b1, hy_w2, hy_b2, hy_w3, hy_sin_freq)
    filt_ctx = _hyena_filters(CTX_LEN, hy_w1, hy_b1, hy_w2, hy_b2, hy_w3, hy_sin_freq)

    offs = [0, 3 * D_HYENA, 3 * D_HYENA + 3 * D_SCONV]
    offs += [offs[2] + D_NA, offs[2] + 2 * D_NA, offs[2] + 3 * D_NA, w_in.shape[2]]

    for i in range(DEPTH):
        last = i == DEPTH - 1
        m = [mods[i, :, j] for j in range(6)]
        w_secs = [bf(w_in[i][:, offs[j]:offs[j + 1]]) for j in range(6)]
        u_hy, u_sc, q, k, v, gates = _proj(xa, m[0], m[1], w_secs, N_TOK)

        z, x0 = _hy_pre(u_hy, hy_conv_w[i], hy_conv_b[i][None], None, SEQ, 0)
        z, x0 = _hy_pre(u_hy, hy_conv_w[i], hy_conv_b[i][None], (z, x0), CTX_LEN, N_LAT)
        yc = jnp.concatenate([_hyena_conv(z, filt_lat[i], SEQ, 0), _hyena_conv(z, filt_ctx[i], CTX_LEN, N_LAT)], axis=0)
        ysc = _short_conv(u_sc, sc_conv_w[i], None, SEQ, 0)
        ysc = _short_conv(u_sc, sc_conv_w[i], ysc, CTX_LEN, N_LAT)
        att = _na_attention(q, k, v, _na_bias(na_rpb[i]))
        att = _ctx_attention(q, k, v, att)

        xa, hmoe, hpk = _mix(yc, z, x0, ysc, att, gates, xa, m[2], m[3], m[4], hy_bias_d[i][None],
                        bf(hy_proj[i]), bf(sc_proj[i]), bf(na_proj[i]), bf(w_o[i]), ln1_g[i][None], ln1_b[i][None], N_TOK)

        n_moe = N_LAT if last else N_TOK
        xa = _moe_layer(hmoe, hpk, xa, m[5], moe_router[i].T, moe_bias[i][:, None], bf(moe_w1[i]), bf(moe_w3[i]),
                        bf(moe_w2[i]), bf(sh_w1[i]), bf(sh_w3[i]), bf(sh_w2[i]), ln2_g[i][None], ln2_b[i][None], n_moe)
    return xa.reshape(BATCH, SEQ, D_MODEL)
```

```python
import functools
import math

import numpy as np
import jax
import jax.numpy as jnp
from jax import lax
from jax.experimental import pallas as pl
from jax.experimental.pallas import tpu as pltpu

F32 = jnp.float32
BF16 = jnp.bfloat16
I32 = jnp.int32
HIGHEST = lax.Precision.HIGHEST

D_MODEL = 1024
BATCH = 8
SEQ = 4096
DEPTH = 4
CTX_LEN = 256
GRID_W = 64
D_HYENA = 512
D_SCONV = 512
NA_HEADS = 8
NA_HEAD_DIM = 64
D_NA = NA_HEADS * NA_HEAD_DIM
NA_WIN_ROWS = 8
NA_WIN_COLS = 16
HY_BANDS = 16
HY_EMB = 1 + 2 * HY_BANDS
HY_FILTER_DIM = 64
HY_FAST_DECAY = 0.3
HY_SLOW_DECAY = 1.5
HY_TARGET = 1e-2
N_EXPERTS = 64
N_GROUPS = 8
TOPK_GROUPS = 4
TOP_K = 8
D_EXPERT = 256
D_SHARED = 256
ROUTED_SCALE = 2.5
LN_EPS = 1e-5
NEG_INF = -1e30
ALPHA = (2.0 * DEPTH) ** 0.25

N_LAT = BATCH * SEQ
N_CTX = BATCH * CTX_LEN
N_TOK = N_LAT + N_CTX

LANE = 128
SUBLANE = 8
VMEM_LIMIT = 56 * 1024 * 1024

TOK_TILE = 512
NA_QROWS = 4
NA_KROWS = NA_QROWS + NA_WIN_ROWS - 1
MOE_BLOCK = 512
RT_TILE = 512
CB_TILE = 128
LC_CH = 8
ROW_UNROLL = 4
WAIT_UNROLL = 16
PACK_W = D_MODEL // 2
TOK_ROWS = PACK_W // LANE
U32 = jnp.uint32


def _cparams(sem):
    return pltpu.CompilerParams(dimension_semantics=sem, vmem_limit_bytes=VMEM_LIMIT)


def _mod_row(tile, tiles_per_batch, n_lat_tiles):
    return jnp.where(tile < n_lat_tiles, tile // tiles_per_batch, BATCH)


def _layer_norm(v, g, b):
    mu = jnp.mean(v, axis=-1, keepdims=True)
    c = v - mu
    var = jnp.mean(c * c, axis=-1, keepdims=True)
    return c * lax.rsqrt(var + LN_EPS) * g + b


def _ada_kernel(c_ref, w_ref, b_ref, o_ref):
    c = c_ref[...]
    cond = c * jax.nn.sigmoid(c)
    o_ref[0] = jnp.dot(cond, w_ref[0], precision=HIGHEST, preferred_element_type=F32) + b_ref[0]


def _ada(cc, w_ada, b_ada):
    depth, d, n = w_ada.shape
    tn = 1536
    return pl.pallas_call(
        _ada_kernel,
        grid=(depth, n // tn),
        in_specs=[
            pl.BlockSpec((16, d), lambda l, j: (0, 0)),
            pl.BlockSpec((1, d, tn), lambda l, j: (l, 0, j)),
            pl.BlockSpec((1, 1, tn), lambda l, j: (l, 0, j)),
        ],
        out_specs=pl.BlockSpec((1, 16, tn), lambda l, j: (l, 0, j)),
        out_shape=jax.ShapeDtypeStruct((depth, 16, n), F32),
        compiler_params=_cparams(("arbitrary", "arbitrary")),
        name="ada",
    )(cc, w_ada, b_ada.reshape(depth, 1, n))


def _proj_kernel(x_ref, sh_ref, sc_ref, w_hy, w_sc, w_q, w_k, w_v, w_g, o_hy, o_sc, o_q, o_k, o_v, o_g):
    h = (x_ref[...] * (1.0 + sc_ref[0]) + sh_ref[0]).astype(BF16)
    for w, o in ((w_hy, o_hy), (w_sc, o_sc), (w_q, o_q), (w_k, o_k), (w_v, o_v), (w_g, o_g)):
        o[...] = jnp.dot(h, w[...], preferred_element_type=F32).astype(o.dtype)


def _proj(x, shift, scale, ws, n_tok):
    tm = TOK_TILE
    nt = n_tok // tm
    mod = lambda t: (_mod_row(t, SEQ // tm, N_LAT // tm), 0, 0)
    widths = [w.shape[1] for w in ws]
    dtypes = [F32, F32, BF16, BF16, BF16, F32]
    return pl.pallas_call(
        _proj_kernel,
        grid=(nt,),
        in_specs=[
            pl.BlockSpec((tm, D_MODEL), lambda t: (t, 0)),
            pl.BlockSpec((1, 1, D_MODEL), mod),
            pl.BlockSpec((1, 1, D_MODEL), mod),
        ] + [pl.BlockSpec((D_MODEL, n), lambda t: (0, 0), pipeline_mode=pl.Buffered(1)) for n in widths],
        out_specs=[pl.BlockSpec((tm, n), lambda t: (t, 0)) for n in widths],
        out_shape=[jax.ShapeDtypeStruct((x.shape[0], n), dt) for n, dt in zip(widths, dtypes)],
        compiler_params=_cparams(("arbitrary",)),
        name="in_proj",
    )(x, shift, scale, *ws)


def _dwconv3(u, w):
    s = u.shape[0]
    row = lax.broadcasted_iota(I32, u.shape, 0)
    prev = jnp.where(row == 0, 0.0, pltpu.roll(u, 1, 0))
    nxt = jnp.where(row == s - 1, 0.0, pltpu.roll(u, s - 1, 0))
    return prev * w[0:1] + u * w[1:2] + nxt * w[2:3]


def _hy_pre_kernel(u0_ref, u1_ref, u2_ref, w0_ref, w1_ref, w2_ref, b0_ref, b1_ref, b2_ref, pz_ref, px_ref,
                   z_ref, x0_ref):
    del pz_ref, px_ref
    x0_ref[...] = _dwconv3(u0_ref[...], w0_ref[...]) + b0_ref[...]
    x1 = _dwconv3(u1_ref[...], w1_ref[...]) + b1_ref[...]
    v = _dwconv3(u2_ref[...], w2_ref[...]) + b2_ref[...]
    z_ref[...] = v * x1


def _hy_pre(u, conv_w, conv_b, prev, seq, row_off):
    ncb = D_HYENA // LANE
    ob = row_off // seq
    usp = lambda s: pl.BlockSpec((seq, LANE), lambda b, c: (ob + b, s * ncb + c))
    wsp = lambda s: pl.BlockSpec((3, LANE), lambda b, c: (0, s * ncb + c))
    bsp = lambda s: pl.BlockSpec((1, LANE), lambda b, c: (0, s * ncb + c))
    osp = pl.BlockSpec((seq, LANE), lambda b, c: (ob + b, c))
    n_tok = u.shape[0]
    if prev is None:
        prev = (jnp.zeros((n_tok, D_HYENA), F32), jnp.zeros((n_tok, D_HYENA), F32))
    args = [u, u, u, conv_w, conv_w, conv_w, conv_b, conv_b, conv_b] + list(prev)
    in_specs = [usp(0), usp(1), usp(2), wsp(0), wsp(1), wsp(2), bsp(0), bsp(1), bsp(2)]
    in_specs += [pl.BlockSpec(memory_space=pl.ANY)] * 2
    return pl.pallas_call(
        _hy_pre_kernel,
        grid=(BATCH, ncb),
        in_specs=in_specs,
        out_specs=[osp, osp],
        out_shape=[jax.ShapeDtypeStruct((n_tok, D_HYENA), F32)] * 2,
        input_output_aliases={9: 0, 10: 1},
        compiler_params=_cparams(("arbitrary", "arbitrary")),
        name="hyena_pre",
    )(*args)


def _sc_kernel(bg_ref, cg_ref, xs_ref, w_ref, prev_ref, o_ref):
    del prev_ref
    o_ref[...] = bg_ref[...] * _dwconv3(cg_ref[...] * xs_ref[...], w_ref[...])


def _short_conv(u, conv_w, prev, seq, row_off):
    ncb = D_SCONV // LANE
    ob = row_off // seq
    usp = lambda s: pl.BlockSpec((seq, LANE), lambda b, c: (ob + b, s * ncb + c))
    osp = pl.BlockSpec((seq, LANE), lambda b, c: (ob + b, c))
    if prev is None:
        prev = jnp.zeros((u.shape[0], D_SCONV), F32)
    in_specs = [usp(0), usp(1), usp(2), pl.BlockSpec((3, LANE), lambda b, c: (0, c)),
                pl.BlockSpec(memory_space=pl.ANY)]
    return pl.pallas_call(
        _sc_kernel,
        grid=(BATCH, ncb),
        in_specs=in_specs,
        out_specs=osp,
        out_shape=jax.ShapeDtypeStruct((u.shape[0], D_SCONV), F32),
        input_output_aliases={4: 0},
        compiler_params=_cparams(("arbitrary", "arbitrary")),
        name="short_conv",
    )(u, u, u, conv_w, prev)


def _filter_kernel(f_ref, dec_ref, w1_ref, b1_ref, w2_ref, b2_ref, w3_ref, fr_ref, o_ref):
    dot = functools.partial(jnp.dot, precision=HIGHEST, preferred_element_type=F32)
    z = jnp.sin(fr_ref[0, 0:1] * (dot(f_ref[...], w1_ref[0]) + b1_ref[0]))
    z = jnp.sin(fr_ref[0, 1:2] * (dot(z, w2_ref[0]) + b2_ref[0]))
    o_ref[0] = dot(z, w3_ref[0]) * dec_ref[...]


def _filter_consts(length):
    t = jnp.linspace(0.0, 1.0, length, dtype=F32)[:, None]
    ang = (2.0 * math.pi / length) * jnp.arange(length, dtype=F32)[:, None]
    bands = jnp.linspace(1e-4, HY_BANDS - 1, HY_BANDS, dtype=F32)[None, :]
    feats = jnp.concatenate([t, jnp.cos(bands * ang), -jnp.sin(bands * ang)], -1)
    feats = jnp.pad(feats, ((0, 0), (0, LANE - HY_EMB)))
    deltas = jnp.abs(jnp.linspace(math.log(HY_TARGET) / HY_SLOW_DECAY, math.log(HY_TARGET) / HY_FAST_DECAY,
                                  D_HYENA, dtype=F32))
    decay = jnp.exp(-t * deltas[None, :])
    return feats, jnp.concatenate([decay, decay], -1)


def _hyena_filters(length, w1, b1, w2, b2, w3, sin_freq):
    depth = w1.shape[0]
    pf = LANE - HY_FILTER_DIM
    w1p = jnp.pad(w1, ((0, 0), (0, LANE - HY_EMB), (0, pf)))
    b1p = jnp.pad(b1, ((0, 0), (0, pf)))[:, None, :]
    w2p = jnp.pad(w2, ((0, 0), (0, pf), (0, pf)))
    b2p = jnp.pad(b2, ((0, 0), (0, pf)))[:, None, :]
    w3p = jnp.pad(w3, ((0, 0), (0, pf), (0, 0)))
    frp = jnp.pad(sin_freq, ((0, 0), (0, 0), (0, pf)))
    feats, decay = _filter_consts(length)
    tl = min(length, 512)
    lsp = lambda shape: pl.BlockSpec((1,) + shape, lambda l, j: (l, 0, 0))
    h = pl.pallas_call(
        _filter_kernel,
        grid=(depth, length // tl),
        in_specs=[
            pl.BlockSpec((tl, LANE), lambda l, j: (j, 0)),
            pl.BlockSpec((tl, 2 * D_HYENA), lambda l, j: (j, 0)),
            lsp((LANE, LANE)), lsp((1, LANE)), lsp((LANE, LANE)), lsp((1, LANE)),
            lsp((LANE, 2 * D_HYENA)), lsp((2, LANE)),
        ],
        out_specs=pl.BlockSpec((1, tl, 2 * D_HYENA), lambda l, j: (l, j, 0)),
        out_shape=jax.ShapeDtypeStruct((depth, length, 2 * D_HYENA), F32),
        compiler_params=_cparams(("arbitrary", "arbitrary")),
        name="hyena_filter",
    )(feats, decay, w1p, b1p, w2p, b2p, w3p, frp)
    h_fwd = h[:, :, :D_HYENA]
    h_bwd = h[:, :, D_HYENA:]
    g_lin = jnp.concatenate([jnp.zeros((depth, 1, D_HYENA), F32), h_bwd[:, :0:-1], h_fwd], axis=1)
    return g_lin.transpose(0, 2, 1).reshape(depth, D_HYENA, 2 * length // LANE, LANE)


def _lconv_kernel(nb, z_ref, g_ref, y_ref, zl_ref):
    krow = lax.broadcasted_iota(I32, (LANE, LANE), 0)
    acol = lax.broadcasted_iota(I32, (LANE, LANE), 1)
    upper = acol >= krow

    def shifted(c, seg):
        return pltpu.roll(jnp.broadcast_to(g_ref[c, seg:seg + 1, :], (LANE, LANE)), 0, 1, stride=1, stride_axis=0)

    def per_channel(c, carry):
        for j in range(nb):
            zl_ref[j * BATCH:(j + 1) * BATCH, :] = z_ref[c, :, j * LANE:(j + 1) * LANE]
        y_ref[c] = jnp.zeros((nb * BATCH, LANE), F32)
        prev = shifted(c, 0)
        for d in range(1 - nb, nb):
            cur = shifted(c, d + nb)
            toep = jnp.where(upper, cur, prev).astype(BF16)
            j0, j1 = max(0, -d), min(nb, nb - d)
            part = jnp.dot(zl_ref[j0 * BATCH:j1 * BATCH, :].astype(BF16), toep, preferred_element_type=F32)
            y_ref[c, (j0 + d) * BATCH:(j1 + d) * BATCH, :] += part
            prev = cur
        return carry

    lax.fori_loop(0, LC_CH, per_channel, 0)


def _long_conv(zt, g):
    ch, _, length = zt.shape
    nb = length // LANE
    return pl.pallas_call(
        functools.partial(_lconv_kernel, nb),
        grid=(ch // LC_CH,),
        in_specs=[
            pl.BlockSpec((LC_CH, BATCH, length), lambda i: (i, 0, 0)),
            pl.BlockSpec((LC_CH, 2 * nb, LANE), lambda i: (i, 0, 0)),
        ],
        out_specs=pl.BlockSpec((LC_CH, nb * BATCH, LANE), lambda i: (i, 0, 0)),
        out_shape=jax.ShapeDtypeStruct((ch, nb * BATCH, LANE), F32),
        scratch_shapes=[pltpu.VMEM((nb * BATCH, LANE), F32)],
        compiler_params=_cparams(("arbitrary",)),
        name="hyena_long_conv",
    )(zt, g)


def _hyena_conv(z, g, seq, row_off):
    zt = z[row_off:row_off + BATCH * seq].reshape(BATCH, seq, D_HYENA).transpose(2, 0, 1)
    yt = _long_conv(zt, g)
    nb = seq // LANE
    return yt.reshape(D_HYENA, nb, BATCH, LANE).transpose(2, 1, 3, 0).reshape(BATCH * seq, D_HYENA)


def _na_bias_index():
    rows = SEQ // GRID_W
    ngrp = rows // NA_QROWS
    qcol = np.arange(GRID_W)
    cstart = np.clip(qcol - NA_WIN_COLS // 2, 0, GRID_W - NA_WIN_COLS)
    ridx = np.zeros((3, NA_QROWS, NA_KROWS), np.int32)
    valid = np.zeros((3, NA_QROWS, GRID_W, NA_KROWS, GRID_W), bool)
    for v, g in enumerate((0, 1, ngrp - 1)):
        u0 = int(np.clip(NA_QROWS * g - NA_WIN_ROWS // 2, 0, rows - NA_KROWS))
        for ri in range(NA_QROWS):
            r = NA_QROWS * g + ri
            rs = int(np.clip(r - NA_WIN_ROWS // 2, 0, rows - NA_WIN_ROWS))
            kr = u0 + np.arange(NA_KROWS)
            row_ok = (kr >= rs) & (kr < rs + NA_WIN_ROWS)
            col_ok = (qcol[None, :] >= cstart[:, None]) & (qcol[None, :] < cstart[:, None] + NA_WIN_COLS)
            ok = row_ok[None, :, None] & col_ok[:, None, :]
            valid[v, ri] = ok
            ridx[v, ri] = np.clip(kr - r + NA_WIN_ROWS - 1, 0, 2 * NA_WIN_ROWS - 2)
    rel = np.clip(qcol[None, :] - qcol[:, None] + NA_WIN_COLS - 1, 0, 2 * NA_WIN_COLS - 2)
    onehot = (rel.reshape(1, -1) == np.arange(2 * NA_WIN_COLS - 1)[:, None]).astype(np.float32)
    nq, nk = NA_QROWS * GRID_W, NA_KROWS * GRID_W
    return ridx, onehot, valid.reshape(3, nq, nk)


def _na_bias(rpb):
    ridx, onehot, valid = _na_bias_index()
    nrow = 2 * NA_WIN_ROWS - 1
    cols = jnp.dot(rpb.reshape(NA_HEADS * nrow, -1), jnp.asarray(onehot), precision=HIGHEST)
    cols = cols.reshape(NA_HEADS, nrow, GRID_W, GRID_W)
    b = cols[:, ridx]
    b = b.transpose(1, 0, 2, 4, 3, 5).reshape(3, NA_HEADS, NA_QROWS * GRID_W, NA_KROWS * GRID_W)
    return jnp.where(valid[:, None], b, NEG_INF)


def _softmax_av(s_list, v_list):
    m = functools.reduce(jnp.maximum, [s.max(axis=-1, keepdims=True) for s in s_list])
    ps = [jnp.exp(s - m) for s in s_list]
    den = functools.reduce(jnp.add, [p.sum(axis=-1, keepdims=True) for p in ps])
    o = functools.reduce(jnp.add, [jnp.dot(p.astype(BF16), v, preferred_element_type=F32) for p, v in zip(ps, v_list)])
    return o / den


def _qk(q, k):
    return lax.dot_general(q, k, (((1,), (1,)), ((), ())), preferred_element_type=F32)


def _na_kernel(q_ref, k_ref, v_ref, kc_ref, vc_ref, bias_ref, prev_ref, o_ref):
    del prev_ref
    rows = SEQ // GRID_W
    g = pl.program_id(1)
    u0 = jnp.clip(NA_QROWS * g - NA_WIN_ROWS // 2, 0, rows - NA_KROWS)
    start = pl.multiple_of(u0 * GRID_W, GRID_W)
    nk = NA_KROWS * GRID_W
    scale = NA_HEAD_DIM ** -0.5
    for h in range(NA_HEADS):
        sl = slice(h * NA_HEAD_DIM, (h + 1) * NA_HEAD_DIM)
        qh = q_ref[:, sl]
        kh = k_ref[pl.ds(start, nk), sl]
        vh = v_ref[pl.ds(start, nk), sl]
        s_loc = _qk(qh, kh) * scale + bias_ref[0, h]
        s_ctx = _qk(qh, kc_ref[:, sl]) * scale
        o_ref[:, sl] = _softmax_av([s_loc, s_ctx], [vh, vc_ref[:, sl]]).astype(o_ref.dtype)


def _na_attention(q, k, v, bias):
    nq = NA_QROWS * GRID_W
    ngrp = SEQ // nq
    ctx0 = N_LAT // CTX_LEN

    def variant(b, g):
        return (jnp.where(g == 0, 0, jnp.where(g == ngrp - 1, 2, 1)), 0, 0, 0)

    return pl.pallas_call(
        _na_kernel,
        grid=(BATCH, ngrp),
        in_specs=[
            pl.BlockSpec((nq, D_NA), lambda b, g: (b * ngrp + g, 0)),
            pl.BlockSpec((SEQ, D_NA), lambda b, g: (b, 0)),
            pl.BlockSpec((SEQ, D_NA), lambda b, g: (b, 0)),
            pl.BlockSpec((CTX_LEN, D_NA), lambda b, g: (ctx0 + b, 0)),
            pl.BlockSpec((CTX_LEN, D_NA), lambda b, g: (ctx0 + b, 0)),
            pl.BlockSpec((1, NA_HEADS, nq, NA_KROWS * GRID_W), variant),
            pl.BlockSpec(memory_space=pl.ANY),
        ],
        out_specs=pl.BlockSpec((nq, D_NA), lambda b, g: (b * ngrp + g, 0)),
        out_shape=jax.ShapeDtypeStruct((N_TOK, D_NA), BF16),
        input_output_aliases={6: 0},
        compiler_params=_cparams(("arbitrary", "arbitrary")),
        name="na_attention",
    )(q, k, v, k, v, bias, jnp.zeros((N_TOK, D_NA), BF16))


def _ctx_attn_kernel(q_ref, k_ref, v_ref, prev_ref, o_ref):
    del prev_ref
    scale = NA_HEAD_DIM ** -0.5
    for h in range(NA_HEADS):
        sl = slice(h * NA_HEAD_DIM, (h + 1) * NA_HEAD_DIM)
        s = _qk(q_ref[:, sl], k_ref[:, sl]) * scale
        o_ref[:, sl] = _softmax_av([s], [v_ref[:, sl]]).astype(o_ref.dtype)


def _ctx_attention(q, k, v, att):
    ctx0 = N_LAT // CTX_LEN
    sp = pl.BlockSpec((CTX_LEN, D_NA), lambda b: (ctx0 + b, 0))
    return pl.pallas_call(
        _ctx_attn_kernel,
        grid=(BATCH,),
        in_specs=[sp, sp, sp, pl.BlockSpec(memory_space=pl.ANY)],
        out_specs=sp,
        out_shape=jax.ShapeDtypeStruct(att.shape, att.dtype),
        input_output_aliases={3: 0},
        compiler_params=_cparams(("arbitrary",)),
        name="ctx_attention",
    )(q, k, v, att)


def _mix_kernel(yc_ref, z_ref, x0_ref, ysc_ref, ya_ref, gt_ref, xl_ref, g1_ref, sh2_ref, sc2_ref, bd_ref,
                wh_ref, ws_ref, wn_ref, wo_ref, lg_ref, lb_ref, xo_ref, ho_ref, hp_ref):
    dot = functools.partial(jnp.dot, preferred_element_type=F32)
    y_hy = x0_ref[...] * (yc_ref[...] + z_ref[...] * bd_ref[...])
    merged = (jax.nn.sigmoid(gt_ref[:, 0:D_MODEL]) * dot(y_hy.astype(BF16), wh_ref[...])
              + jax.nn.sigmoid(gt_ref[:, D_MODEL:2 * D_MODEL]) * dot(ysc_ref[...].astype(BF16), ws_ref[...])
              + jax.nn.sigmoid(gt_ref[:, 2 * D_MODEL:3 * D_MODEL]) * dot(ya_ref[...], wn_ref[...]))
    out = dot(merged.astype(BF16), wo_ref[...])
    xo = _layer_norm(ALPHA * xl_ref[...] + g1_ref[0] * out, lg_ref[...], lb_ref[...])
    xo_ref[...] = xo
    ho = xo * (1.0 + sc2_ref[0]) + sh2_ref[0]
    ho_ref[...] = ho
    _store_words(hp_ref, _pack_pairs(ho), ho.shape[0])


def _mix(yc, z, x0, ysc, ya, gates, xl, g1, sh2, sc2, bias_d, wh, ws, wn, wo, ln_g, ln_b, n_tok):
    tm = TOK_TILE
    tok = lambda n: pl.BlockSpec((tm, n), lambda t: (t, 0))
    mod = pl.BlockSpec((1, 1, D_MODEL), lambda t: (_mod_row(t, SEQ // tm, N_LAT // tm), 0, 0))
    const = lambda r, n: pl.BlockSpec((r, n), lambda t: (0, 0))
    return pl.pallas_call(
        _mix_kernel,
        grid=(n_tok // tm,),
        in_specs=[tok(D_HYENA), tok(D_HYENA), tok(D_HYENA), tok(D_SCONV), tok(D_NA), tok(3 * D_MODEL), tok(D_MODEL),
                  mod, mod, mod, const(1, D_HYENA),
                  const(D_HYENA, D_MODEL), const(D_SCONV, D_MODEL), const(D_NA, D_MODEL), const(D_MODEL, D_MODEL),
                  const(1, D_MODEL), const(1, D_MODEL)],
        out_specs=[tok(D_MODEL), tok(D_MODEL), pl.BlockSpec((tm * TOK_ROWS, LANE), lambda t: (t, 0))],
        out_shape=[jax.ShapeDtypeStruct((xl.shape[0], D_MODEL), F32)] * 2
        + [jax.ShapeDtypeStruct((xl.shape[0] * TOK_ROWS, LANE), U32)],
        compiler_params=_cparams(("arbitrary",)),
        name="mixer_out",
    )(yc, z, x0, ysc, ya, gates, xl, g1, sh2, sc2, bias_d, wh, ws, wn, wo, ln_g, ln_b)


def _router_kernel(h_ref, wr_ref, rb_ref, tri_ref, te_ref, rk_ref, w_ref, cnt_ref, run_ref):
    tm = h_ref.shape[0]
    per = N_EXPERTS // N_GROUPS

    @pl.when(pl.program_id(0) == 0)
    def _():
        run_ref[...] = jnp.zeros_like(run_ref)

    logits = lax.dot_general(wr_ref[...], h_ref[...], (((1,), (1,)), ((), ())),
                             precision=HIGHEST, preferred_element_type=F32)
    scores = jax.nn.sigmoid(logits)
    sel = scores + rb_ref[...]
    sub = lax.broadcasted_iota(I32, (per, tm), 0)
    colmax = lambda a: jnp.max(a, axis=0, keepdims=True)
    colmin = lambda a: jnp.min(a, axis=0, keepdims=True)
    ninf = -jnp.inf

    xs = [sel[g * per:(g + 1) * per, :] for g in range(N_GROUPS)]
    sc = [scores[g * per:(g + 1) * per, :] for g in range(N_GROUPS)]
    gs = []
    for x in xs:
        m1 = colmax(x)
        i1 = colmin(jnp.where(x == m1, sub, per))
        m2 = colmax(jnp.where(sub == i1, ninf, x))
        gs.append(m1 + m2)
    chosen = [jnp.zeros((1, tm), F32) for _ in range(N_GROUPS)]
    for _ in range(TOPK_GROUPS):
        gm = functools.reduce(jnp.maximum, gs)
        gi = jnp.full((1, tm), N_GROUPS, I32)
        for g in reversed(range(N_GROUPS)):
            gi = jnp.where(gs[g] == gm, g, gi)
        for g in range(N_GROUPS):
            hit = gi == g
            chosen[g] = jnp.where(hit, 1.0, chosen[g])
            gs[g] = jnp.where(hit, ninf, gs[g])
    xm = [jnp.where(jnp.broadcast_to(chosen[g], (per, tm)) > 0.5, xs[g], ninf) for g in range(N_GROUPS)]
    eidx = [sub + g * per for g in range(N_GROUPS)]
    picked = [jnp.zeros((per, tm), F32) for _ in range(N_GROUPS)]
    top_e, top_s = [], []
    for _ in range(TOP_K):
        em = functools.reduce(jnp.maximum, [colmax(x) for x in xm])
        ei = functools.reduce(jnp.minimum, [colmin(jnp.where(xm[g] == em, eidx[g], N_EXPERTS)) for g in range(N_GROUPS)])
        s_acc = jnp.zeros((1, tm), F32)
        for g in range(N_GROUPS):
            hit = eidx[g] == ei
            picked[g] = jnp.where(hit, 1.0, picked[g])
            xm[g] = jnp.where(hit, ninf, xm[g])
            s_acc = s_acc + jnp.sum(jnp.where(hit, sc[g], 0.0), axis=0, keepdims=True)
        top_e.append(ei)
        top_s.append(s_acc)
    den = functools.reduce(jnp.add, top_s)
    mask = jnp.concatenate(picked, axis=0)
    prefix = jnp.dot(mask.astype(BF16), tri_ref[...], preferred_element_type=F32)
    pos = run_ref[:, 0:1] + prefix
    posg = [pos[g * per:(g + 1) * per, :] for g in range(N_GROUPS)]
    for i in range(TOP_K):
        r_acc = jnp.zeros((1, tm), F32)
        for g in range(N_GROUPS):
            r_acc = r_acc + jnp.sum(jnp.where(eidx[g] == top_e[i], posg[g], 0.0), axis=0, keepdims=True)
        te_ref[0, i:i + 1, :] = top_e[i]
        rk_ref[0, i:i + 1, :] = r_acc.astype(I32)
        w_ref[0, i:i + 1, :] = top_s[i] / den * ROUTED_SCALE
    run_ref[...] = run_ref[...] + jnp.sum(mask, axis=1, keepdims=True)
    cnt_ref[...] = run_ref[...]


def _route(tokens, router_w_t, router_b, n_tok):
    tm = RT_TILE
    nt = n_tok // tm
    tri = jnp.asarray(np.triu(np.ones((tm, tm), np.float32), 1), BF16)
    out3 = lambda dt: jax.ShapeDtypeStruct((nt, TOP_K, tm), dt)
    osp = pl.BlockSpec((1, TOP_K, tm), lambda t: (t, 0, 0))
    return pl.pallas_call(
        _router_kernel,
        grid=(nt,),
        in_specs=[
            pl.BlockSpec((tm, D_MODEL), lambda t: (t, 0)),
            pl.BlockSpec((N_EXPERTS, D_MODEL), lambda t: (0, 0)),
            pl.BlockSpec((N_EXPERTS, 1), lambda t: (0, 0)),
            pl.BlockSpec((tm, tm), lambda t: (0, 0)),
        ],
        out_specs=[osp, osp, osp, pl.BlockSpec((N_EXPERTS, LANE), lambda t: (0, 0))],
        out_shape=[out3(I32), out3(I32), out3(F32), jax.ShapeDtypeStruct((N_EXPERTS, LANE), F32)],
        scratch_shapes=[pltpu.VMEM((N_EXPERTS, LANE), F32)],
        compiler_params=_cparams(("arbitrary",)),
        name="moe_router",
    )(tokens, router_w_t, router_b, tri)


def _pack_pairs(v):
    bits = lambda a: lax.bitcast_convert_type(a.astype(BF16).astype(F32), U32)
    return (bits(v[:, PACK_W:]) & jnp.uint32(0xFFFF0000)) | (bits(v[:, :PACK_W]) >> 16)


def _unpack_pairs(w):
    lo = lax.bitcast_convert_type(w << 16, F32)
    hi = lax.bitcast_convert_type(w & jnp.uint32(0xFFFF0000), F32)
    return lo, hi


def _tok_rows(t):
    return pl.ds(pl.multiple_of(t * TOK_ROWS, TOK_ROWS), TOK_ROWS)


def _load_words(ref, n, lead=()):
    return jnp.concatenate([ref[lead + (pl.ds(j, n, stride=TOK_ROWS), slice(None))] for j in range(TOK_ROWS)], axis=-1)


def _store_words(ref, words, n):
    for j in range(TOK_ROWS):
        ref[pl.ds(j, n, stride=TOK_ROWS), :] = words[:, j * LANE:(j + 1) * LANE]


def _dispatch_kernel(cnt_ref, pst_ref, na_ref, dest_ref, h_ref, xs_hbm, zero_ref, sem):
    tm = dest_ref.shape[2]
    row_copy = lambda src, dst: pltpu.make_async_copy(src, xs_hbm.at[_tok_rows(dst)], sem)
    blk_rows = MOE_BLOCK * TOK_ROWS
    n_blocks = xs_hbm.shape[0] // blk_rows

    @pl.when(pl.program_id(0) == 0)
    def _():
        zero_ref[...] = jnp.zeros_like(zero_ref)

        def blk_copy(blk):
            return pltpu.make_async_copy(zero_ref, xs_hbm.at[pl.ds(pl.multiple_of(blk * blk_rows, blk_rows), blk_rows)], sem)

        def fill_blk(blk, c):
            blk_copy(blk).start()
            return c

        def drain_blk(blk, c):
            blk_copy(blk).wait()
            return c

        lax.fori_loop(na_ref[0], n_blocks, fill_blk, 0)
        lax.fori_loop(na_ref[0], n_blocks, drain_blk, 0)

        def per_expert(e, carry):
            n = cnt_ref[e]
            n_pad = (n + MOE_BLOCK - 1) // MOE_BLOCK * MOE_BLOCK - n
            base = pst_ref[e] + n

            def fill(r, c):
                row_copy(zero_ref.at[_tok_rows(0)], base + r).start()
                return c

            def drain(r, c):
                row_copy(zero_ref.at[_tok_rows(0)], base).wait()
                return c

            lax.fori_loop(0, n_pad, fill, 0)
            lax.fori_loop(0, n_pad, drain, 0)
            return carry

        lax.fori_loop(0, N_EXPERTS, per_expert, 0)

    def issue(tb, carry):
        for tt in range(ROW_UNROLL):
            t = tb * ROW_UNROLL + tt
            for i in range(TOP_K):
                row_copy(h_ref.at[_tok_rows(t)], dest_ref[0, i, t]).start(priority=i % 2)
        return carry

    def drain(tb, carry):
        for _ in range(WAIT_UNROLL * TOP_K):
            row_copy(h_ref.at[_tok_rows(0)], 0).wait()
        return carry

    lax.fori_loop(0, tm // ROW_UNROLL, issue, 0)
    lax.fori_loop(0, tm // WAIT_UNROLL, drain, 0)


def _dispatch(counts, pstart, n_active, dest, tokens, n_slots):
    nt, _, tm = dest.shape
    grid_spec = pltpu.PrefetchScalarGridSpec(
        num_scalar_prefetch=3,
        grid=(nt,),
        in_specs=[
            pl.BlockSpec((1, TOP_K, tm), lambda t, c, p, a: (t, 0, 0), memory_space=pltpu.SMEM),
            pl.BlockSpec((tm * TOK_ROWS, LANE), lambda t, c, p, a: (t, 0)),
        ],
        out_specs=pl.BlockSpec(memory_space=pl.ANY),
        scratch_shapes=[pltpu.VMEM((MOE_BLOCK * TOK_ROWS, LANE), U32), pltpu.SemaphoreType.DMA],
    )
    return pl.pallas_call(
        _dispatch_kernel,
        grid_spec=grid_spec,
        out_shape=jax.ShapeDtypeStruct((n_slots * TOK_ROWS, LANE), U32),
        compiler_params=_cparams(("arbitrary",)),
        name="moe_dispatch",
    )(counts, pstart, n_active, dest, tokens)


def _expert_kernel(be_ref, na_ref, x_ref, w1_ref, w3_ref, w2_ref, y_ref):
    active = pl.program_id(0) < na_ref[0]

    @pl.when(active)
    def _():
        dot = functools.partial(jnp.dot, preferred_element_type=F32)
        lo, hi = _unpack_pairs(_load_words(x_ref, MOE_BLOCK))
        lo, hi = lo.astype(BF16), hi.astype(BF16)
        xw = lambda w_ref: dot(lo, w_ref[0, :PACK_W, :]) + dot(hi, w_ref[0, PACK_W:, :])
        a = xw(w1_ref)
        hidden = (a * jax.nn.sigmoid(a)) * xw(w3_ref)
        _store_words(y_ref, _pack_pairs(dot(hidden.astype(BF16), w2_ref[0])), MOE_BLOCK)

    @pl.when(jnp.logical_not(active))
    def _():
        y_ref[...] = jnp.zeros_like(y_ref)


def _experts(blk_expert, n_active, xs, w1, w3, w2):
    blk_rows = MOE_BLOCK * TOK_ROWS
    n_blocks = xs.shape[0] // blk_rows
    xmap = lambda i, be, na: (jnp.minimum(i, na[0] - 1), 0)
    wmap = lambda i, be, na: (be[i], 0, 0)
    grid_spec = pltpu.PrefetchScalarGridSpec(
        num_scalar_prefetch=2,
        grid=(n_blocks,),
        in_specs=[
            pl.BlockSpec((blk_rows, LANE), xmap),
            pl.BlockSpec((1, D_MODEL, D_EXPERT), wmap),
            pl.BlockSpec((1, D_MODEL, D_EXPERT), wmap),
            pl.BlockSpec((1, D_EXPERT, D_MODEL), wmap),
        ],
        out_specs=pl.BlockSpec((blk_rows, LANE), lambda i, be, na: (i, 0)),
    )
    return pl.pallas_call(
        _expert_kernel,
        grid_spec=grid_spec,
        out_shape=jax.ShapeDtypeStruct(xs.shape, U32),
        compiler_params=_cparams(("arbitrary",)),
        name="moe_experts",
    )(blk_expert, n_active, xs, w1, w3, w2)


def _combine_kernel(dest_ref, w_ref, h_ref, xl_ref, g2_ref, s1_ref, s3_ref, s2_ref, lg_ref, lb_ref, ys_hbm,
                    xo_ref, ybuf, sem):
    tm = h_ref.shape[0]

    def row_copy(i, t, src):
        return pltpu.make_async_copy(ys_hbm.at[_tok_rows(src)], ybuf.at[i, _tok_rows(t)], sem)

    def issue(tb, carry):
        for tt in range(ROW_UNROLL):
            t = tb * ROW_UNROLL + tt
            for i in range(TOP_K):
                row_copy(i, t, dest_ref[0, i, t]).start(priority=i % 2)
        return carry

    def drain(tb, carry):
        for _ in range(WAIT_UNROLL * TOP_K):
            row_copy(0, 0, 0).wait()
        return carry

    lax.fori_loop(0, tm // ROW_UNROLL, issue, 0)
    dot = functools.partial(jnp.dot, preferred_element_type=F32)
    hb = h_ref[...].astype(BF16)
    a = dot(hb, s1_ref[...])
    f = dot(((a * jax.nn.sigmoid(a)) * dot(hb, s3_ref[...])).astype(BF16), s2_ref[...])
    lax.fori_loop(0, tm // WAIT_UNROLL, drain, 0)
    r_lo = jnp.zeros((tm, PACK_W), F32)
    r_hi = jnp.zeros((tm, PACK_W), F32)
    for i in range(TOP_K):
        lo, hi = _unpack_pairs(_load_words(ybuf, tm, (i,)))
        r_lo = r_lo + w_ref[:, i:i + 1] * lo
        r_hi = r_hi + w_ref[:, i:i + 1] * hi
    f = f + jnp.concatenate([r_lo, r_hi], axis=-1)
    xo_ref[...] = _layer_norm(ALPHA * xl_ref[...] + g2_ref[0] * f, lg_ref[...], lb_ref[...])


def _combine(dest, w, h, xl, g2, s1, s3, s2, ln_g, ln_b, ys, n_tok):
    tm = CB_TILE
    nt = n_tok // tm
    tok = lambda n: pl.BlockSpec((tm, n), lambda t: (t, 0))
    const = lambda r, n: pl.BlockSpec((r, n), lambda t: (0, 0))
    return pl.pallas_call(
        _combine_kernel,
        grid=(nt,),
        in_specs=[
            pl.BlockSpec((1, TOP_K, tm), lambda t: (t, 0, 0), memory_space=pltpu.SMEM),
            tok(TOP_K), tok(D_MODEL), tok(D_MODEL),
            pl.BlockSpec((1, 1, D_MODEL), lambda t: (_mod_row(t, SEQ // tm, N_LAT // tm), 0, 0)),
            const(D_MODEL, D_SHARED), const(D_MODEL, D_SHARED), const(D_SHARED, D_MODEL),
            const(1, D_MODEL), const(1, D_MODEL),
            pl.BlockSpec(memory_space=pl.ANY),
        ],
        out_specs=tok(D_MODEL),
        out_shape=jax.ShapeDtypeStruct((n_tok, D_MODEL), F32),
        scratch_shapes=[pltpu.VMEM((TOP_K, tm * TOK_ROWS, LANE), U32), pltpu.SemaphoreType.DMA],
        compiler_params=_cparams(("arbitrary",)),
        name="moe_combine",
    )(dest, w, h, xl, g2, s1, s3, s2, ln_g, ln_b, ys)


def _moe_layer(h, h_packed, xl, g2, router_w_t, router_b, w1, w3, w2, s1, s3, s2, ln_g, ln_b, n_tok):
    top_e, rank, wts, cnt = _route(h, router_w_t, router_b, n_tok)
    counts = cnt[:, 0].astype(I32)
    padded = (counts + MOE_BLOCK - 1) // MOE_BLOCK * MOE_BLOCK
    pend = jnp.cumsum(padded)
    pstart = pend - padded
    n_blocks = n_tok * TOP_K // MOE_BLOCK + N_EXPERTS
    onehot = top_e[..., None] == jnp.arange(N_EXPERTS, dtype=I32)
    dest = rank + jnp.sum(jnp.where(onehot, pstart, 0), axis=-1)
    blk_expert = jnp.minimum(
        jnp.sum((pend // MOE_BLOCK)[None, :] <= jnp.arange(n_blocks, dtype=I32)[:, None], axis=1), N_EXPERTS - 1).astype(I32)
    n_active = (pend[-1:] // MOE_BLOCK).astype(I32)
    xs = _dispatch(counts, pstart.astype(I32), n_active, dest, h_packed, n_blocks * MOE_BLOCK)
    ys = _experts(blk_expert, n_active, xs, w1, w3, w2)
    nt = n_tok // CB_TILE
    regroup = lambda a: a.transpose(1, 0, 2).reshape(TOP_K, n_tok)
    dest_c = regroup(dest).reshape(TOP_K, nt, CB_TILE).transpose(1, 0, 2)
    w_c = regroup(wts).T
    return _combine(dest_c, w_c, h, xl, g2, s1, s3, s2, ln_g, ln_b, ys, n_tok)


def kernel(x, c, ctx, c_ctx, w_ada, b_ada, w_in, hy_conv_w, hy_conv_b, hy_w1, hy_b1, hy_w2, hy_b2, hy_w3, hy_sin_freq, hy_bias_d, hy_proj, sc_conv_w, sc_proj, na_rpb, na_proj, w_o, ln1_g, ln1_b, ln2_g, ln2_b, moe_router, moe_bias, moe_w1, moe_w3, moe_w2, sh_w1, sh_w3, sh_w2):
    bf = lambda a: a.astype(BF16)
    xa = jnp.concatenate([x.reshape(N_LAT, D_MODEL), ctx.reshape(N_CTX, D_MODEL)], axis=0)

    cc = jnp.zeros((16, D_MODEL), F32).at[:BATCH].set(c).at[BATCH].set(c_ctx)
    mods = _ada(cc, w_ada, b_ada)[:, :BATCH + 1].reshape(DEPTH, BATCH + 1, 6, 1, D_MODEL)
    filt_lat = _hyena_filters(SEQ, hy_w1, hy_b1, hy_w2, hy_b2, hy_w3, hy_sin_freq)
    filt_ctx = _hyena_filters(CTX_LEN, hy_w1, hy_b1, hy_w2, hy_b2, hy_w3, hy_sin_freq)

    offs = [0, 3 * D_HYENA, 3 * D_HYENA + 3 * D_SCONV]
    offs += [offs[2] + D_NA, offs[2] + 2 * D_NA, offs[2] + 3 * D_NA, w_in.shape[2]]

    for i in range(DEPTH):
        last = i == DEPTH - 1
        m = [mods[i, :, j] for j in range(6)]
        w_secs = [bf(w_in[i][:, offs[j]:offs[j + 1]]) for j in range(6)]
        u_hy, u_sc, q, k, v, gates = _proj(xa, m[0], m[1], w_secs, N_TOK)

        z, x0 = _hy_pre(u_hy, hy_conv_w[i], hy_conv_b[i][None], None, SEQ, 0)
        z, x0 = _hy_pre(u_hy, hy_conv_w[i], hy_conv_b[i][None], (z, x0), CTX_LEN, N_LAT)
        yc = jnp.concatenate([_hyena_conv(z, filt_lat[i], SEQ, 0), _hyena_conv(z, filt_ctx[i], CTX_LEN, N_LAT)], axis=0)
        ysc = _short_conv(u_sc, sc_conv_w[i], None, SEQ, 0)
        ysc = _short_conv(u_sc, sc_conv_w[i], ysc, CTX_LEN, N_LAT)
        att = _na_attention(q, k, v, _na_bias(na_rpb[i]))
        att = _ctx_attention(q, k, v, att)

        xa, hmoe, hpk = _mix(yc, z, x0, ysc, att, gates, xa, m[2], m[3], m[4], hy_bias_d[i][None],
                        bf(hy_proj[i]), bf(sc_proj[i]), bf(na_proj[i]), bf(w_o[i]), ln1_g[i][None], ln1_b[i][None], N_TOK)

        n_moe = N_LAT if last else N_TOK
        xa = _moe_layer(hmoe, hpk, xa, m[5], moe_router[i].T, moe_bias[i][:, None], bf(moe_w1[i]), bf(moe_w3[i]),
                        bf(moe_w2[i]), bf(sh_w1[i]), bf(sh_w3[i]), bf(sh_w2[i]), ln2_g[i][None], ln2_b[i][None], n_moe)
    return xa.reshape(BATCH, SEQ, D_MODEL)
```
